```python
import jax, jax.numpy as jnp
from jax import lax
import numpy as np

D_MODEL = 1024
BATCH = 4
SEQ = 8192
DEPTH = 4

CHUNK = 64
N_A_LAYERS = DEPTH // 2
N_B_LAYERS = DEPTH - N_A_LAYERS
EXPAND = 2
D_INNER = EXPAND * D_MODEL
POOL_WINDOWS = (2, 4, 8, 16)
N_POOL_GROUPS = len(POOL_WINDOWS)
POOL_GROUP_W = D_INNER // N_POOL_GROUPS
N_HEADS = 16
HEAD_DIM = D_INNER // N_HEADS
LEFT_CHUNKS = 8
BAND = (LEFT_CHUNKS + 1) * CHUNK
REL_CLIP = 128
EPS = 1e-6

kernel_name = "yoco_pool_chunkattn_adaln_trunk"


def rms_norm(x, g):
    xf = x.astype(jnp.float32)
    y = xf * lax.rsqrt(jnp.mean(xf * xf, axis=-1, keepdims=True) + EPS)
    return (y * g.astype(jnp.float32)).astype(x.dtype)


def modulate(h, shift, scale):
    return h * (1 + scale[:, None, :]) + shift[:, None, :]


def multiscale_pool(v):
    B, S, _ = v.shape
    vf = v.astype(jnp.float32).reshape(B, S, N_POOL_GROUPS, POOL_GROUP_W)
    cs = jnp.cumsum(vf, axis=1)
    t = jnp.arange(S)
    outs = []
    for g, w in enumerate(POOL_WINDOWS):
        csg = cs[:, :, g]
        lagged = jnp.pad(csg, ((0, 0), (w, 0), (0, 0)))[:, :S]
        cnt = jnp.minimum(t + 1, w).astype(jnp.float32)[None, :, None]
        outs.append((csg - lagged) / cnt - vf[:, :, g])
    return jnp.stack(outs, axis=2)


def chunk_band_attention(q, k, v, rel_bias):
    B, S, H, Dh = q.shape
    n_chunks = S // CHUNK
    pad = LEFT_CHUNKS * CHUNK
    k_pad = jnp.pad(k, ((0, 0), (pad, 0), (0, 0), (0, 0)))
    v_pad = jnp.pad(v, ((0, 0), (pad, 0), (0, 0), (0, 0)))
    qi = jnp.arange(CHUNK)[:, None]
    kj = jnp.arange(BAND)[None, :]
    dist = pad + qi - kj
    idx = jnp.clip(dist, -REL_CLIP, REL_CLIP) + REL_CLIP
    bias = rel_bias[:, idx].astype(jnp.float32)
    sm_scale = HEAD_DIM ** -0.5
    band_pos = jnp.arange(BAND) - pad

    def one_chunk(n):
        start = n * CHUNK
        qb = lax.dynamic_slice_in_dim(q, start, CHUNK, axis=1)
        kb = lax.dynamic_slice_in_dim(k_pad, start, BAND, axis=1)
        vb = lax.dynamic_slice_in_dim(v_pad, start, BAND, axis=1)
        s = jnp.einsum('bqhd,bkhd->bhqk', qb, kb).astype(jnp.float32) * sm_scale + bias[None]
        valid = (start + band_pos) >= 0
        s = jnp.where(valid[None, None, None, :], s, -jnp.inf)
        p = jax.nn.softmax(s, axis=-1)
        return jnp.einsum('bhqk,bkhd->bqhd', p.astype(vb.dtype), vb)

    out = lax.map(one_chunk, jnp.arange(n_chunks))
    return out.transpose(1, 0, 2, 3, 4).reshape(B, S, H * Dh)


def setup_inputs(seed: int = 0) -> dict:
    key = jax.random.key(seed)
    ks = jax.random.split(key, 20)
    D, E, Gw = D_MODEL, D_INNER, POOL_GROUP_W
    nrm = jax.random.normal
    return {
        "x": nrm(ks[0], (BATCH, SEQ, D), jnp.float32),
        "c": nrm(ks[1], (BATCH, D), jnp.float32),
        "ada_w": nrm(ks[2], (DEPTH, D, 3 * D), jnp.float32) * D ** -0.5,
        "ada_b": 0.01 * nrm(ks[3], (DEPTH, 3 * D), jnp.float32),
        "norm_g": 1.0 + 0.05 * nrm(ks[4], (DEPTH, D), jnp.float32),
        "a_w_in": nrm(ks[5], (N_A_LAYERS, D, 2 * E), jnp.float32) * D ** -0.5,
        "a_w_group": nrm(ks[6], (N_A_LAYERS, N_POOL_GROUPS, Gw, Gw), jnp.float32) * Gw ** -0.5,
        "a_scale": 1.0 + 0.1 * nrm(ks[7], (N_A_LAYERS, E), jnp.float32),
        "a_w_out": nrm(ks[8], (N_A_LAYERS, E, D), jnp.float32) * E ** -0.5,
        "kv_norm_g": 1.0 + 0.05 * nrm(ks[9], (D,), jnp.float32),
        "kv_ada_w": nrm(ks[10], (D, 2 * D), jnp.float32) * D ** -0.5,
        "kv_ada_b": 0.01 * nrm(ks[11], (2 * D,), jnp.float32),
        "w_kv": nrm(ks[12], (D, 2 * E), jnp.float32) * D ** -0.5,
        "b_w_in": nrm(ks[13], (N_B_LAYERS, D, 2 * E), jnp.float32) * D ** -0.5,
        "b_rel_bias": 0.5 * nrm(ks[14], (N_B_LAYERS, N_HEADS, 2 * REL_CLIP + 1), jnp.float32),
        "b_w_out": nrm(ks[15], (N_B_LAYERS, E, D), jnp.float32) * E ** -0.5,
        "final_g": 1.0 + 0.05 * nrm(ks[16], (D,), jnp.float32),
    }


def reference(x, c, ada_w, ada_b, norm_g, a_w_in, a_w_group, a_scale, a_w_out,
              kv_norm_g, kv_ada_w, kv_ada_b, w_kv, b_w_in, b_rel_bias, b_w_out,
              final_g):
    B, S, _ = x.shape
    c_act = jax.nn.silu(c)
    h = x
    k = v = None
    for layer in range(DEPTH):
        shift, scale, gate = jnp.split(c_act @ ada_w[layer] + ada_b[layer], 3, axis=-1)
        u = modulate(rms_norm(h, norm_g[layer]), shift, scale)
        if layer < N_A_LAYERS:
            a = layer
            val, z = jnp.split(u @ a_w_in[a], 2, axis=-1)
            pooled = multiscale_pool(val).astype(val.dtype)
            mixed = jnp.einsum('bsgi,gio->bsgo', pooled, a_w_group[a]).reshape(B, S, D_INNER)
            mixed = mixed * a_scale[a]
            y = (mixed * jax.nn.silu(z)) @ a_w_out[a]
        else:
            if layer == N_A_LAYERS:
                kv_shift, kv_scale = jnp.split(c_act @ kv_ada_w + kv_ada_b, 2, axis=-1)
                hk = modulate(rms_norm(h, kv_norm_g), kv_shift, kv_scale)
                k, v = jnp.split(hk @ w_kv, 2, axis=-1)
                k = k.reshape(B, S, N_HEADS, HEAD_DIM)
                v = v.reshape(B, S, N_HEADS, HEAD_DIM)
            bi = layer - N_A_LAYERS
            qv, z = jnp.split(u @ b_w_in[bi], 2, axis=-1)
            q = qv.reshape(B, S, N_HEADS, HEAD_DIM)
            att = chunk_band_attention(q, k, v, b_rel_bias[bi])
            y = (att * jax.nn.silu(z)) @ b_w_out[bi]
        h = h + gate[:, None, :] * y
    return rms_norm(h, final_g)
```

```python
import functools

import jax
import jax.numpy as jnp
from jax import lax
from jax.experimental import pallas as pl
from jax.experimental.pallas import tpu as pltpu

F32 = jnp.float32
BF16 = jnp.bfloat16

D_MODEL = 1024
D_INNER = 2048
N_HEADS = 16
HEAD_DIM = 128
CHUNK = 64
LEFT_CHUNKS = 8
REL_CLIP = 128
POOL_WINDOWS = (2, 4, 8, 16)
POOL_GROUP_W = D_INNER // len(POOL_WINDOWS)
POOL_HALO = 16
EPS = 1e-6
SM_SCALE = HEAD_DIM ** -0.5
NEG_INF = float("-inf")

LANES = 128
Q_BLOCK = 256
KEY_BLOCKS = LEFT_CHUNKS * CHUNK // Q_BLOCK + 1
A_TILE = 512
P_TILE = 512
VMEM_LIMIT = 54 * 1024 * 1024


def _silu(x):
    return x * (1.0 / (1.0 + jnp.exp(-x)))


def _rms_scale(x):
    return lax.rsqrt(jnp.mean(x * x, axis=-1, keepdims=True) + EPS)


def _resident(shape):
    return pl.BlockSpec(shape, lambda *_: (0,) * len(shape), pipeline_mode=pl.Buffered(1))


def _ada_kernel(c_ref, w_ref, b_ref, o_ref):
    ca = _silu(c_ref[...])
    o_ref[0] = jnp.dot(ca, w_ref[0], preferred_element_type=F32) + b_ref[0]


def _ada_call(c_pad, w, b):
    n_layers, d, n = w.shape
    tn = 1024
    return pl.pallas_call(
        _ada_kernel,
        out_shape=jax.ShapeDtypeStruct((n_layers, c_pad.shape[0], n), F32),
        grid=(n_layers, n // tn),
        in_specs=[
            pl.BlockSpec(c_pad.shape, lambda l, j: (0, 0)),
            pl.BlockSpec((1, d, tn), lambda l, j: (l, 0, j)),
            pl.BlockSpec((1, 1, tn), lambda l, j: (l, 0, j)),
        ],
        out_specs=pl.BlockSpec((1, c_pad.shape[0], tn), lambda l, j: (l, 0, j)),
        compiler_params=pltpu.CompilerParams(
            dimension_semantics=("arbitrary", "arbitrary")),
        name="ada_mod",
    )(c_pad, w, b)


def _a_kernel(h_ref, sh_ref, sc_ref, gt_ref, g_ref, win_ref, wg_ref, asc_ref, wout_ref,
              o_ref, carry_ref):
    ts = h_ref.shape[1]
    si = pl.program_id(1)
    x = h_ref[0]
    u = (x * _rms_scale(x) * g_ref[...]) * (1.0 + sc_ref[0]) + sh_ref[0]
    vz = jnp.dot(u.astype(BF16), win_ref[...], preferred_element_type=F32)
    val = vz[:, :D_INNER]
    z = vz[:, D_INNER:]

    @pl.when(si == 0)
    def _():
        carry_ref[...] = jnp.zeros_like(carry_ref)

    ext = jnp.concatenate([carry_ref[...], val], axis=0)
    carry_ref[...] = val[ts - POOL_HALO:, :]

    t = si * ts + lax.broadcasted_iota(jnp.int32, (ts, LANES), 0)
    mixed = []
    for g, w in enumerate(POOL_WINDOWS):
        lo, hi = g * POOL_GROUP_W, (g + 1) * POOL_GROUP_W
        s = ext[:, lo:hi]
        k = 1
        while k < w:
            s = s + pltpu.roll(s, k, axis=0)
            k *= 2
        s = s[POOL_HALO:, :]
        inv = 1.0 / jnp.minimum(t + 1, w).astype(F32)
        inv = jnp.concatenate([inv] * (POOL_GROUP_W // LANES), axis=1)
        pooled = s * inv - val[:, lo:hi]
        mixed.append(jnp.dot(pooled.astype(BF16), wg_ref[g], preferred_element_type=F32))
    mixed = jnp.concatenate(mixed, axis=1)
    gz = (mixed * asc_ref[...]) * _silu(z)
    y = jnp.dot(gz.astype(BF16), wout_ref[...], preferred_element_type=F32)
    o_ref[0] = x + gt_ref[0] * y


def _a_layer_call(h, shift, scale, gate, g, w_in, w_group, a_scale, w_out):
    b, s, d = h.shape
    ts = A_TILE
    row = pl.BlockSpec((1, ts, d), lambda bi, si: (bi, si, 0))
    vec = pl.BlockSpec((1, 1, d), lambda bi, si: (bi, 0, 0))
    return pl.pallas_call(
        _a_kernel,
        out_shape=jax.ShapeDtypeStruct(h.shape, F32),
        grid=(b, s // ts),
        in_specs=[row, vec, vec, vec,
                  _resident((1, d)), _resident(w_in.shape), _resident(w_group.shape),
                  _resident((1, D_INNER)), _resident(w_out.shape)],
        out_specs=row,
        scratch_shapes=[pltpu.VMEM((POOL_HALO, D_INNER), F32)],
        compiler_params=pltpu.CompilerParams(
            dimension_semantics=("arbitrary", "arbitrary"), vmem_limit_bytes=VMEM_LIMIT),
        name="a_layer",
    )(h, shift, scale, gate, g, w_in, w_group, a_scale, w_out)


def _store_heads(dst_ref, val):
    for hd in range(N_HEADS):
        dst_ref[0, hd] = val[:, hd * HEAD_DIM:(hd + 1) * HEAD_DIM].astype(BF16)


def _bproj_kernel(*refs, with_kv):
    if with_kv:
        (h_ref, sh_ref, sc_ref, g_ref, win_ref, ksh_ref, ksc_ref, kg_ref, wkv_ref,
         q_ref, z_ref, k_ref, v_ref) = refs
    else:
        h_ref, sh_ref, sc_ref, g_ref, win_ref, q_ref, z_ref = refs
    x = h_ref[0]
    xn = x * _rms_scale(x)
    u = (xn * g_ref[...]) * (1.0 + sc_ref[0]) + sh_ref[0]
    qz = jnp.dot(u.astype(BF16), win_ref[...], preferred_element_type=F32)
    _store_heads(q_ref, qz[:, :D_INNER])
    z_ref[0] = qz[:, D_INNER:]
    if with_kv:
        hk = (xn * kg_ref[...]) * (1.0 + ksc_ref[0]) + ksh_ref[0]
        kv = jnp.dot(hk.astype(BF16), wkv_ref[...], preferred_element_type=F32)
        _store_heads(k_ref, kv[:, :D_INNER])
        _store_heads(v_ref, kv[:, D_INNER:])


def _bproj_call(h, shift, scale, g, w_in, kv=None):
    b, s, d = h.shape
    ts = P_TILE
    row = pl.BlockSpec((1, ts, d), lambda bi, si: (bi, si, 0))
    vec = pl.BlockSpec((1, 1, d), lambda bi, si: (bi, 0, 0))
    heads = pl.BlockSpec((1, N_HEADS, ts, HEAD_DIM), lambda bi, si: (bi, 0, si, 0))
    heads_shape = jax.ShapeDtypeStruct((b, N_HEADS, s, HEAD_DIM), BF16)
    in_specs = [row, vec, vec, _resident((1, d)), _resident(w_in.shape)]
    args = [h, shift, scale, g, w_in]
    out_shape = [heads_shape, jax.ShapeDtypeStruct((b, s, D_INNER), F32)]
    out_specs = [heads, pl.BlockSpec((1, ts, D_INNER), lambda bi, si: (bi, si, 0))]
    if kv is not None:
        kshift, kscale, kg, w_kv = kv
        in_specs += [vec, vec, _resident((1, d)), _resident(w_kv.shape)]
        args += [kshift, kscale, kg, w_kv]
        out_shape += [heads_shape, heads_shape]
        out_specs += [heads, heads]
    return pl.pallas_call(
        functools.partial(_bproj_kernel, with_kv=kv is not None),
        out_shape=out_shape,
        grid=(b, s // ts),
        in_specs=in_specs,
        out_specs=out_specs,
        compiler_params=pltpu.CompilerParams(
            dimension_semantics=("arbitrary", "arbitrary"), vmem_limit_bytes=VMEM_LIMIT),
        name="b_proj_kv" if kv is not None else "b_proj",
    )(*args)


def _attn_kernel(*refs, final):
    if final:
        (q_ref, k_ref, v_ref, z_ref, h_ref, gt_ref, tab_ref, wout_ref, fg_ref,
         o_ref, kring, vring, att_ref) = refs
    else:
        (q_ref, k_ref, v_ref, z_ref, h_ref, gt_ref, tab_ref, wout_ref,
         o_ref, kring, vring, att_ref) = refs
    i = pl.program_id(1)

    @pl.when(i == 0)
    def _():
        kring[...] = jnp.zeros_like(kring)
        vring[...] = jnp.zeros_like(vring)

    slot = lax.rem(i, KEY_BLOCKS)
    kring[slot] = k_ref[0]
    vring[slot] = v_ref[0]

    def one_head(hd):
        qh = q_ref[0, hd]
        scores = []
        for j in range(KEY_BLOCKS):
            sl = lax.rem(i + 1 + j, KEY_BLOCKS)
            s = lax.dot_general(qh, kring[sl, hd], (((1,), (1,)), ((), ())),
                                preferred_element_type=F32)
            s = s * SM_SCALE + tab_ref[hd, :, j * Q_BLOCK:(j + 1) * Q_BLOCK]
            if j < KEY_BLOCKS - 1:
                s = jnp.where(i >= KEY_BLOCKS - 1 - j, s, NEG_INF)
            scores.append(s)
        m = functools.reduce(jnp.maximum, scores).max(axis=-1, keepdims=True)
        acc = None
        den = None
        for j in range(KEY_BLOCKS):
            sl = lax.rem(i + 1 + j, KEY_BLOCKS)
            p = jnp.exp(scores[j] - m)
            pj = jnp.sum(p, axis=-1, keepdims=True)
            oj = jnp.dot(p.astype(BF16), vring[sl, hd], preferred_element_type=F32)
            den = pj if den is None else den + pj
            acc = oj if acc is None else acc + oj
        att_ref[hd] = acc * (1.0 / den)

    def pair(hp, carry):
        one_head(2 * hp)
        one_head(2 * hp + 1)
        return carry

    lax.fori_loop(0, N_HEADS // 2, pair, 0)

    att = jnp.concatenate([att_ref[hd] for hd in range(N_HEADS)], axis=1)
    gz = att * _silu(z_ref[0])
    y = jnp.dot(gz.astype(BF16), wout_ref[...], preferred_element_type=F32)
    out = h_ref[0] + gt_ref[0] * y
    if final:
        out = out * _rms_scale(out) * fg_ref[...]
    o_ref[0] = out


def _attn_call(q, k, v, z, h, gate, table, w_out, final_g=None):
    b, s, d = h.shape
    tq = Q_BLOCK
    heads = pl.BlockSpec((1, N_HEADS, tq, HEAD_DIM), lambda bi, i: (bi, 0, i, 0))
    row = pl.BlockSpec((1, tq, d), lambda bi, i: (bi, i, 0))
    in_specs = [heads, heads, heads,
                pl.BlockSpec((1, tq, D_INNER), lambda bi, i: (bi, i, 0)),
                row,
                pl.BlockSpec((1, 1, d), lambda bi, i: (bi, 0, 0)),
                _resident(table.shape), _resident(w_out.shape)]
    args = [q, k, v, z, h, gate, table, w_out]
    if final_g is not None:
        in_specs.append(_resident((1, d)))
        args.append(final_g)
    ring = pltpu.VMEM((KEY_BLOCKS, N_HEADS, tq, HEAD_DIM), BF16)
    return pl.pallas_call(
        functools.partial(_attn_kernel, final=final_g is not None),
        out_shape=jax.ShapeDtypeStruct(h.shape, F32),
        grid=(b, s // tq),
        in_specs=in_specs,
        out_specs=row,
        scratch_shapes=[ring, ring, pltpu.VMEM((N_HEADS, tq, HEAD_DIM), F32)],
        compiler_params=pltpu.CompilerParams(
            dimension_semantics=("arbitrary", "arbitrary"), vmem_limit_bytes=VMEM_LIMIT),
        name="band_attn_final" if final_g is not None else "band_attn",
    )(*args)


def _bias_table(rel_bias):
    ql = jnp.arange(Q_BLOCK)[:, None]
    kl = jnp.arange(KEY_BLOCKS * Q_BLOCK)[None, :]
    dist = (KEY_BLOCKS - 1) * Q_BLOCK + ql - kl
    idx = jnp.clip(dist, -REL_CLIP, REL_CLIP) + REL_CLIP
    qc, kc = ql // CHUNK, kl // CHUNK
    in_band = (kc >= qc) & (kc <= qc + LEFT_CHUNKS)
    return jnp.where(in_band[None], rel_bias[:, idx].astype(F32), NEG_INF)


@jax.jit
def _forward(x, c, ada_w, ada_b, norm_g, a_w_in, a_w_group, a_scale, a_w_out, kv_norm_g,
             kv_ada_w, kv_ada_b, w_kv, b_w_in, b_rel_bias, b_w_out, final_g):
    b, s, d = x.shape
    depth = ada_w.shape[0]
    n_a = a_w_in.shape[0]
    c_pad = jnp.pad(c, ((0, 8 - b), (0, 0)))
    mods = _ada_call(c_pad, ada_w, ada_b.reshape(depth, 1, 3 * d))[:, :b]
    kvm = _ada_call(c_pad, kv_ada_w[None], kv_ada_b.reshape(1, 1, 2 * d))[0, :b]

    def vec(a):
        return a.reshape(b, 1, d)

    h = x
    k = v = None
    for layer in range(depth):
        shift, scale, gate = (vec(mods[layer, :, j * d:(j + 1) * d]) for j in range(3))
        g = norm_g[layer].reshape(1, d)
        if layer < n_a:
            h = _a_layer_call(h, shift, scale, gate, g,
                              a_w_in[layer].astype(BF16), a_w_group[layer].astype(BF16),
                              a_scale[layer].reshape(1, D_INNER), a_w_out[layer].astype(BF16))
        else:
            bi = layer - n_a
            w_in = b_w_in[bi].astype(BF16)
            if layer == n_a:
                kv = (vec(kvm[:, :d]), vec(kvm[:, d:]), kv_norm_g.reshape(1, d),
                      w_kv.astype(BF16))
                q, z, k, v = _bproj_call(h, shift, scale, g, w_in, kv)
            else:
                q, z = _bproj_call(h, shift, scale, g, w_in)
            last = layer == depth - 1
            h = _attn_call(q, k, v, z, h, gate, _bias_table(b_rel_bias[bi]),
                           b_w_out[bi].astype(BF16),
                           final_g.reshape(1, d) if last else None)
    return h


def kernel(x, c, ada_w, ada_b, norm_g, a_w_in, a_w_group, a_scale, a_w_out, kv_norm_g, kv_ada_w, kv_ada_b, w_kv, b_w_in, b_rel_bias, b_w_out, final_g):
    return _forward(x, c, ada_w, ada_b, norm_g, a_w_in, a_w_group, a_scale, a_w_out,
                    kv_norm_g, kv_ada_w, kv_ada_b, w_kv, b_w_in, b_rel_bias, b_w_out, final_g)
```

```python
import functools

import jax
import jax.numpy as jnp
from jax import lax
from jax.experimental import pallas as pl
from jax.experimental.pallas import tpu as pltpu

F32 = jnp.float32
BF16 = jnp.bfloat16

D_MODEL = 1024
D_INNER = 2048
N_HEADS = 16
HEAD_DIM = 128
CHUNK = 64
LEFT_CHUNKS = 8
REL_CLIP = 128
POOL_WINDOWS = (2, 4, 8, 16)
POOL_GROUP_W = D_INNER // len(POOL_WINDOWS)
POOL_HALO = 16
EPS = 1e-6
LOG2E = 1.4426950408889634
Q_PRESCALE = HEAD_DIM ** -0.5 * LOG2E
NEG_INF = float("-inf")

LANES = 128
Q_BLOCK = 256
KEY_BLOCKS = LEFT_CHUNKS * CHUNK // Q_BLOCK + 1
A_TILE = 512
P_TILE = 512
VMEM_LIMIT = 54 * 1024 * 1024


def _silu(x):
    return x * (1.0 / (1.0 + jnp.exp(-x)))


def _rms_scale(x):
    return lax.rsqrt(jnp.mean(x * x, axis=-1, keepdims=True) + EPS)


def _resident(shape):
    return pl.BlockSpec(shape, lambda *_: (0,) * len(shape), pipeline_mode=pl.Buffered(1))


def _ada_kernel(c_ref, w_ref, b_ref, o_ref):
    ca = _silu(c_ref[...])
    o_ref[0] = jnp.dot(ca, w_ref[0], preferred_element_type=F32) + b_ref[0]


def _ada_call(c_pad, w, b):
    n_layers, d, n = w.shape
    tn = 1024
    return pl.pallas_call(
        _ada_kernel,
        out_shape=jax.ShapeDtypeStruct((n_layers, c_pad.shape[0], n), F32),
        grid=(n_layers, n // tn),
        in_specs=[
            pl.BlockSpec(c_pad.shape, lambda l, j: (0, 0)),
            pl.BlockSpec((1, d, tn), lambda l, j: (l, 0, j)),
            pl.BlockSpec((1, 1, tn), lambda l, j: (l, 0, j)),
        ],
        out_specs=pl.BlockSpec((1, c_pad.shape[0], tn), lambda l, j: (l, 0, j)),
        compiler_params=pltpu.CompilerParams(
            dimension_semantics=("arbitrary", "arbitrary")),
        name="ada_mod",
    )(c_pad, w, b)


def _a_kernel(h_ref, sh_ref, sc_ref, gt_ref, g_ref, win_ref, wg_ref, asc_ref, wout_ref,
              o_ref, carry_ref):
    ts = h_ref.shape[1]
    si = pl.program_id(1)
    x = h_ref[0]
    u = (x * _rms_scale(x) * g_ref[...]) * (1.0 + sc_ref[0]) + sh_ref[0]
    vz = jnp.dot(u.astype(BF16), win_ref[...], preferred_element_type=F32)
    val = vz[:, :D_INNER]
    z = vz[:, D_INNER:]

    @pl.when(si == 0)
    def _():
        carry_ref[...] = jnp.zeros_like(carry_ref)

    ext = jnp.concatenate([carry_ref[...], val], axis=0)
    carry_ref[...] = val[ts - POOL_HALO:, :]

    t = si * ts + lax.broadcasted_iota(jnp.int32, (ts, LANES), 0)
    mixed = []
    for g, w in enumerate(POOL_WINDOWS):
        lo, hi = g * POOL_GROUP_W, (g + 1) * POOL_GROUP_W
        s = ext[:, lo:hi]
        k = 1
        while k < w:
            s = s + pltpu.roll(s, k, axis=0)
            k *= 2
        s = s[POOL_HALO:, :]
        inv = 1.0 / jnp.minimum(t + 1, w).astype(F32)
        inv = jnp.concatenate([inv] * (POOL_GROUP_W // LANES), axis=1)
        pooled = s * inv - val[:, lo:hi]
        mixed.append(jnp.dot(pooled.astype(BF16), wg_ref[g], preferred_element_type=F32))
    mixed = jnp.concatenate(mixed, axis=1)
    gz = (mixed * asc_ref[...]) * _silu(z)
    y = jnp.dot(gz.astype(BF16), wout_ref[...], preferred_element_type=F32)
    o_ref[0] = x + gt_ref[0] * y


def _a_layer_call(h, shift, scale, gate, g, w_in, w_group, a_scale, w_out):
    b, s, d = h.shape
    ts = A_TILE
    row = pl.BlockSpec((1, ts, d), lambda bi, si: (bi, si, 0))
    vec = pl.BlockSpec((1, 1, d), lambda bi, si: (bi, 0, 0))
    return pl.pallas_call(
        _a_kernel,
        out_shape=jax.ShapeDtypeStruct(h.shape, F32),
        grid=(b, s // ts),
        in_specs=[row, vec, vec, vec,
                  _resident((1, d)), _resident(w_in.shape), _resident(w_group.shape),
                  _resident((1, D_INNER)), _resident(w_out.shape)],
        out_specs=row,
        scratch_shapes=[pltpu.VMEM((POOL_HALO, D_INNER), F32)],
        compiler_params=pltpu.CompilerParams(
            dimension_semantics=("arbitrary", "arbitrary"), vmem_limit_bytes=VMEM_LIMIT),
        name="a_layer",
    )(h, shift, scale, gate, g, w_in, w_group, a_scale, w_out)


def _store_heads(dst_ref, val):
    for hd in range(N_HEADS):
        dst_ref[0, hd] = val[:, hd * HEAD_DIM:(hd + 1) * HEAD_DIM].astype(BF16)


def _bproj_kernel(*refs, with_kv):
    if with_kv:
        (h_ref, sh_ref, sc_ref, g_ref, win_ref, ksh_ref, ksc_ref, kg_ref, wkv_ref,
         q_ref, z_ref, k_ref, v_ref) = refs
    else:
        h_ref, sh_ref, sc_ref, g_ref, win_ref, q_ref, z_ref = refs
    x = h_ref[0]
    xn = x * _rms_scale(x)
    u = (xn * g_ref[...]) * (1.0 + sc_ref[0]) + sh_ref[0]
    qz = jnp.dot(u.astype(BF16), win_ref[...], preferred_element_type=F32)
    _store_heads(q_ref, qz[:, :D_INNER] * Q_PRESCALE)
    z_ref[0] = qz[:, D_INNER:]
    if with_kv:
        hk = (xn * kg_ref[...]) * (1.0 + ksc_ref[0]) + ksh_ref[0]
        kv = jnp.dot(hk.astype(BF16), wkv_ref[...], preferred_element_type=F32)
        _store_heads(k_ref, kv[:, :D_INNER])
        _store_heads(v_ref, kv[:, D_INNER:])


def _bproj_call(h, shift, scale, g, w_in, kv=None):
    b, s, d = h.shape
    ts = P_TILE
    row = pl.BlockSpec((1, ts, d), lambda bi, si: (bi, si, 0))
    vec = pl.BlockSpec((1, 1, d), lambda bi, si: (bi, 0, 0))
    heads = pl.BlockSpec((1, N_HEADS, ts, HEAD_DIM), lambda bi, si: (bi, 0, si, 0))
    heads_shape = jax.ShapeDtypeStruct((b, N_HEADS, s, HEAD_DIM), BF16)
    in_specs = [row, vec, vec, _resident((1, d)), _resident(w_in.shape)]
    args = [h, shift, scale, g, w_in]
    out_shape = [heads_shape, jax.ShapeDtypeStruct((b, s, D_INNER), F32)]
    out_specs = [heads, pl.BlockSpec((1, ts, D_INNER), lambda bi, si: (bi, si, 0))]
    if kv is not None:
        kshift, kscale, kg, w_kv = kv
        in_specs += [vec, vec, _resident((1, d)), _resident(w_kv.shape)]
        args += [kshift, kscale, kg, w_kv]
        out_shape += [heads_shape, heads_shape]
        out_specs += [heads, heads]
    return pl.pallas_call(
        functools.partial(_bproj_kernel, with_kv=kv is not None),
        out_shape=out_shape,
        grid=(b, s // ts),
        in_specs=in_specs,
        out_specs=out_specs,
        compiler_params=pltpu.CompilerParams(
            dimension_semantics=("arbitrary", "arbitrary"), vmem_limit_bytes=VMEM_LIMIT),
        name="b_proj_kv" if kv is not None else "b_proj",
    )(*args)


def _attn_kernel(*refs, final):
    if final:
        (q_ref, k_ref, v_ref, z_ref, h_ref, gt_ref, tab_ref, wout_ref, fg_ref,
         o_ref, kring, vring, att_ref, s_scr, p_scr, r_scr) = refs
    else:
        (q_ref, k_ref, v_ref, z_ref, h_ref, gt_ref, tab_ref, wout_ref,
         o_ref, kring, vring, att_ref, s_scr, p_scr, r_scr) = refs
    i = pl.program_id(1)

    @pl.when(i == 0)
    def _():
        kring[...] = jnp.zeros_like(kring)
        vring[...] = jnp.zeros_like(vring)

    kring[lax.rem(i, KEY_BLOCKS)] = k_ref[0]
    vring[lax.rem(i, KEY_BLOCKS)] = v_ref[0]
    slots = [lax.rem(i + 1 + j, KEY_BLOCKS) for j in range(KEY_BLOCKS)]
    cols = [slice(j * Q_BLOCK, (j + 1) * Q_BLOCK) for j in range(KEY_BLOCKS)]

    def scores(hd, buf, masked):
        qh = q_ref[0, hd]
        for j in range(KEY_BLOCKS):
            s = lax.dot_general(qh, kring[slots[j], hd], (((1,), (1,)), ((), ())),
                                preferred_element_type=F32)
            s = s + tab_ref[hd, :, cols[j]]
            if masked and j < KEY_BLOCKS - 1:
                s = jnp.where(i >= KEY_BLOCKS - 1 - j, s, NEG_INF)
            s_scr[buf, :, cols[j]] = s

    def softmax(buf):
        s = [s_scr[buf, :, cols[j]] for j in range(KEY_BLOCKS)]
        m = functools.reduce(jnp.maximum, s).max(axis=-1, keepdims=True)
        den = None
        for j in range(KEY_BLOCKS):
            p = jnp.exp2(s[j] - m)
            pj = jnp.sum(p, axis=-1, keepdims=True)
            den = pj if den is None else den + pj
            p_scr[buf, :, cols[j]] = p.astype(BF16)
        r_scr[buf] = jnp.broadcast_to(1.0 / den, (Q_BLOCK, HEAD_DIM))

    def values(hd, buf):
        acc = None
        for j in range(KEY_BLOCKS):
            oj = jnp.dot(p_scr[buf, :, cols[j]], vring[slots[j], hd],
                         preferred_element_type=F32)
            acc = oj if acc is None else acc + oj
        att_ref[hd] = acc * r_scr[buf]

    def all_heads(masked):
        scores(0, 0, masked)
        scores(1, 1, masked)
        softmax(0)

        def step(it, carry):
            for u in range(2):
                hd = 1 + 2 * it + u
                scores(hd + 1, u, masked)
                softmax(1 - u)
                values(hd - 1, u)
            return carry

        lax.fori_loop(0, (N_HEADS - 2) // 2, step, 0)
        softmax(1)
        values(N_HEADS - 2, 0)
        values(N_HEADS - 1, 1)

    @pl.when(i >= KEY_BLOCKS - 1)
    def _():
        all_heads(masked=False)

    @pl.when(i < KEY_BLOCKS - 1)
    def _():
        all_heads(masked=True)

    att = jnp.concatenate([att_ref[hd] for hd in range(N_HEADS)], axis=1)
    gz = att * _silu(z_ref[0])
    y = jnp.dot(gz.astype(BF16), wout_ref[...], preferred_element_type=F32)
    out = h_ref[0] + gt_ref[0] * y
    if final:
        out = out * _rms_scale(out) * fg_ref[...]
    o_ref[0] = out


def _attn_call(q, k, v, z, h, gate, table, w_out, final_g=None):
    b, s, d = h.shape
    tq = Q_BLOCK
    heads = pl.BlockSpec((1, N_HEADS, tq, HEAD_DIM), lambda bi, i: (bi, 0, i, 0))
    row = pl.BlockSpec((1, tq, d), lambda bi, i: (bi, i, 0))
    in_specs = [heads, heads, heads,
                pl.BlockSpec((1, tq, D_INNER), lambda bi, i: (bi, i, 0)),
                row,
                pl.BlockSpec((1, 1, d), lambda bi, i: (bi, 0, 0)),
                _resident(table.shape), _resident(w_out.shape)]
    args = [q, k, v, z, h, gate, table, w_out]
    if final_g is not None:
        in_specs.append(_resident((1, d)))
        args.append(final_g)
    ring = pltpu.VMEM((KEY_BLOCKS, N_HEADS, tq, HEAD_DIM), BF16)
    return pl.pallas_call(
        functools.partial(_attn_kernel, final=final_g is not None),
        out_shape=jax.ShapeDtypeStruct(h.shape, F32),
        grid=(b, s // tq),
        in_specs=in_specs,
        out_specs=row,
        scratch_shapes=[ring, ring, pltpu.VMEM((N_HEADS, tq, HEAD_DIM), F32),
                        pltpu.VMEM((2, tq, KEY_BLOCKS * tq), F32),
                        pltpu.VMEM((2, tq, KEY_BLOCKS * tq), BF16),
                        pltpu.VMEM((2, tq, HEAD_DIM), F32)],
        compiler_params=pltpu.CompilerParams(
            dimension_semantics=("arbitrary", "arbitrary"), vmem_limit_bytes=VMEM_LIMIT),
        name="band_attn_final" if final_g is not None else "band_attn",
    )(*args)


def _bias_table(rel_bias):
    n_heads = rel_bias.shape[0]
    n_keys = KEY_BLOCKS * Q_BLOCK
    top = (KEY_BLOCKS - 1) * Q_BLOCK + Q_BLOCK - 1
    n_dist = Q_BLOCK + n_keys - 1
    vec = jnp.concatenate([
        jnp.broadcast_to(rel_bias[:, -1:], (n_heads, top - REL_CLIP)),
        rel_bias[:, ::-1],
        jnp.broadcast_to(rel_bias[:, :1], (n_heads, n_dist - (top - REL_CLIP) - rel_bias.shape[1])),
        jnp.zeros((n_heads, 1), rel_bias.dtype)], axis=1)
    skew = jnp.broadcast_to(vec[:, None, :], (n_heads, Q_BLOCK, n_dist + 1))
    skew = skew.reshape(n_heads, -1)[:, :Q_BLOCK * n_dist].reshape(n_heads, Q_BLOCK, n_dist)
    table = skew[:, :, Q_BLOCK - 1:Q_BLOCK - 1 + n_keys]
    qc = jnp.arange(Q_BLOCK)[:, None] // CHUNK
    kc = jnp.arange(n_keys)[None, :] // CHUNK
    in_band = (kc >= qc) & (kc <= qc + LEFT_CHUNKS)
    return jnp.where(in_band[None], table.astype(F32) * LOG2E, NEG_INF)


@jax.jit
def _forward(x, c, ada_w, ada_b, norm_g, a_w_in, a_w_group, a_scale, a_w_out, kv_norm_g,
             kv_ada_w, kv_ada_b, w_kv, b_w_in, b_rel_bias, b_w_out, final_g):
    b, s, d = x.shape
    depth = ada_w.shape[0]
    n_a = a_w_in.shape[0]
    c_pad = jnp.pad(c, ((0, 8 - b), (0, 0)))
    mods = _ada_call(c_pad, ada_w, ada_b.reshape(depth, 1, 3 * d))[:, :b]
    kvm = _ada_call(c_pad, kv_ada_w[None], kv_ada_b.reshape(1, 1, 2 * d))[0, :b]

    def vec(a):
        return a.reshape(b, 1, d)

    h = x
    k = v = None
    for layer in range(depth):
        shift, scale, gate = (vec(mods[layer, :, j * d:(j + 1) * d]) for j in range(3))
        g = norm_g[layer].reshape(1, d)
        if layer < n_a:
            h = _a_layer_call(h, shift, scale, gate, g,
                              a_w_in[layer].astype(BF16), a_w_group[layer].astype(BF16),
                              a_scale[layer].reshape(1, D_INNER), a_w_out[layer].astype(BF16))
        else:
            bi = layer - n_a
            w_in = b_w_in[bi].astype(BF16)
            if layer == n_a:
                kv = (vec(kvm[:, :d]), vec(kvm[:, d:]), kv_norm_g.reshape(1, d),
                      w_kv.astype(BF16))
                q, z, k, v = _bproj_call(h, shift, scale, g, w_in, kv)
            else:
                q, z = _bproj_call(h, shift, scale, g, w_in)
            last = layer == depth - 1
            h = _attn_call(q, k, v, z, h, gate, _bias_table(b_rel_bias[bi]),
                           b_w_out[bi].astype(BF16),
                           final_g.reshape(1, d) if last else None)
    return h


def kernel(x, c, ada_w, ada_b, norm_g, a_w_in, a_w_group, a_scale, a_w_out, kv_norm_g, kv_ada_w, kv_ada_b, w_kv, b_w_in, b_rel_bias, b_w_out, final_g):
    return _forward(x, c, ada_w, ada_b, norm_g, a_w_in, a_w_group, a_scale, a_w_out,
                    kv_norm_g, kv_ada_w, kv_ada_b, w_kv, b_w_in, b_rel_bias, b_w_out, final_g)
```

```python
import functools

import jax
import jax.numpy as jnp
from jax import lax
from jax.experimental import pallas as pl
from jax.experimental.pallas import tpu as pltpu

F32 = jnp.float32
BF16 = jnp.bfloat16

D_MODEL = 1024
D_INNER = 2048
N_HEADS = 16
HEAD_DIM = 128
CHUNK = 64
LEFT_CHUNKS = 8
REL_CLIP = 128
POOL_WINDOWS = (2, 4, 8, 16)
POOL_GROUP_W = D_INNER // len(POOL_WINDOWS)
POOL_HALO = 16
EPS = 1e-6
LOG2E = 1.4426950408889634
Q_PRESCALE = HEAD_DIM ** -0.5 * LOG2E
NEG_INF = float("-inf")

LANES = 128
Q_BLOCK = 256
KEY_BLOCKS = LEFT_CHUNKS * CHUNK // Q_BLOCK + 1
SOFTMAX_ROWS = 64
A_TILE = 512
P_TILE = 512
VMEM_LIMIT = 54 * 1024 * 1024


def _silu(x):
    return x * (1.0 / (1.0 + jnp.exp(-x)))


def _rms_scale(x):
    return lax.rsqrt(jnp.mean(x * x, axis=-1, keepdims=True) + EPS)


def _resident(shape):
    return pl.BlockSpec(shape, lambda *_: (0,) * len(shape), pipeline_mode=pl.Buffered(1))


def _ada_kernel(c_ref, w_ref, b_ref, o_ref):
    ca = _silu(c_ref[...])
    o_ref[0] = jnp.dot(ca, w_ref[0], preferred_element_type=F32) + b_ref[0]


def _ada_call(c_pad, w, b):
    n_layers, d, n = w.shape
    tn = 1024
    return pl.pallas_call(
        _ada_kernel,
        out_shape=jax.ShapeDtypeStruct((n_layers, c_pad.shape[0], n), F32),
        grid=(n_layers, n // tn),
        in_specs=[
            pl.BlockSpec(c_pad.shape, lambda l, j: (0, 0)),
            pl.BlockSpec((1, d, tn), lambda l, j: (l, 0, j)),
            pl.BlockSpec((1, 1, tn), lambda l, j: (l, 0, j)),
        ],
        out_specs=pl.BlockSpec((1, c_pad.shape[0], tn), lambda l, j: (l, 0, j)),
        compiler_params=pltpu.CompilerParams(
            dimension_semantics=("arbitrary", "arbitrary")),
        name="ada_mod",
    )(c_pad, w, b)


def _a_kernel(h_ref, sh_ref, sc_ref, gt_ref, g_ref, win_ref, wg_ref, asc_ref, wout_ref,
              o_ref, carry_ref):
    ts = h_ref.shape[1]
    si = pl.program_id(1)
    x = h_ref[0]
    u = (x * _rms_scale(x) * g_ref[...]) * (1.0 + sc_ref[0]) + sh_ref[0]
    vz = jnp.dot(u.astype(BF16), win_ref[...], preferred_element_type=F32)
    val = vz[:, :D_INNER]
    z = vz[:, D_INNER:]

    @pl.when(si == 0)
    def _():
        carry_ref[...] = jnp.zeros_like(carry_ref)

    ext = jnp.concatenate([carry_ref[...], val], axis=0)
    carry_ref[...] = val[ts - POOL_HALO:, :]

    t = si * ts + lax.broadcasted_iota(jnp.int32, (ts, LANES), 0)
    mixed = []
    for g, w in enumerate(POOL_WINDOWS):
        lo, hi = g * POOL_GROUP_W, (g + 1) * POOL_GROUP_W
        s = ext[:, lo:hi]
        k = 1
        while k < w:
            s = s + pltpu.roll(s, k, axis=0)
            k *= 2
        s = s[POOL_HALO:, :]
        inv = 1.0 / jnp.minimum(t + 1, w).astype(F32)
        inv = jnp.concatenate([inv] * (POOL_GROUP_W // LANES), axis=1)
        pooled = s * inv - val[:, lo:hi]
        mixed.append(jnp.dot(pooled.astype(BF16), wg_ref[g], preferred_element_type=F32))
    mixed = jnp.concatenate(mixed, axis=1)
    gz = (mixed * asc_ref[...]) * _silu(z)
    y = jnp.dot(gz.astype(BF16), wout_ref[...], preferred_element_type=F32)
    o_ref[0] = x + gt_ref[0] * y


def _a_layer_call(h, shift, scale, gate, g, w_in, w_group, a_scale, w_out):
    b, s, d = h.shape
    ts = A_TILE
    row = pl.BlockSpec((1, ts, d), lambda bi, si: (bi, si, 0))
    vec = pl.BlockSpec((1, 1, d), lambda bi, si: (bi, 0, 0))
    return pl.pallas_call(
        _a_kernel,
        out_shape=jax.ShapeDtypeStruct(h.shape, F32),
        grid=(b, s // ts),
        in_specs=[row, vec, vec, vec,
                  _resident((1, d)), _resident(w_in.shape), _resident(w_group.shape),
                  _resident((1, D_INNER)), _resident(w_out.shape)],
        out_specs=row,
        scratch_shapes=[pltpu.VMEM((POOL_HALO, D_INNER), F32)],
        compiler_params=pltpu.CompilerParams(
            dimension_semantics=("arbitrary", "arbitrary"), vmem_limit_bytes=VMEM_LIMIT),
        name="a_layer",
    )(h, shift, scale, gate, g, w_in, w_group, a_scale, w_out)


def _store_heads(dst_ref, val):
    for hd in range(N_HEADS):
        dst_ref[0, hd] = val[:, hd * HEAD_DIM:(hd + 1) * HEAD_DIM].astype(BF16)


def _bproj_kernel(*refs, with_kv):
    if with_kv:
        (h_ref, sh_ref, sc_ref, g_ref, win_ref, ksh_ref, ksc_ref, kg_ref, wkv_ref,
         q_ref, z_ref, k_ref, v_ref) = refs
    else:
        h_ref, sh_ref, sc_ref, g_ref, win_ref, q_ref, z_ref = refs
    x = h_ref[0]
    xn = x * _rms_scale(x)
    u = (xn * g_ref[...]) * (1.0 + sc_ref[0]) + sh_ref[0]
    qz = jnp.dot(u.astype(BF16), win_ref[...], preferred_element_type=F32)
    _store_heads(q_ref, qz[:, :D_INNER] * Q_PRESCALE)
    z_ref[0] = qz[:, D_INNER:]
    if with_kv:
        hk = (xn * kg_ref[...]) * (1.0 + ksc_ref[0]) + ksh_ref[0]
        kv = jnp.dot(hk.astype(BF16), wkv_ref[...], preferred_element_type=F32)
        _store_heads(k_ref, kv[:, :D_INNER])
        _store_heads(v_ref, kv[:, D_INNER:])


def _bproj_call(h, shift, scale, g, w_in, kv=None):
    b, s, d = h.shape
    ts = P_TILE
    row = pl.BlockSpec((1, ts, d), lambda bi, si: (bi, si, 0))
    vec = pl.BlockSpec((1, 1, d), lambda bi, si: (bi, 0, 0))
    heads = pl.BlockSpec((1, N_HEADS, ts, HEAD_DIM), lambda bi, si: (bi, 0, si, 0))
    heads_shape = jax.ShapeDtypeStruct((b, N_HEADS, s, HEAD_DIM), BF16)
    in_specs = [row, vec, vec, _resident((1, d)), _resident(w_in.shape)]
    args = [h, shift, scale, g, w_in]
    out_shape = [heads_shape, jax.ShapeDtypeStruct((b, s, D_INNER), F32)]
    out_specs = [heads, pl.BlockSpec((1, ts, D_INNER), lambda bi, si: (bi, si, 0))]
    if kv is not None:
        kshift, kscale, kg, w_kv = kv
        in_specs += [vec, vec, _resident((1, d)), _resident(w_kv.shape)]
        args += [kshift, kscale, kg, w_kv]
        out_shape += [heads_shape, heads_shape]
        out_specs += [heads, heads]
    return pl.pallas_call(
        functools.partial(_bproj_kernel, with_kv=kv is not None),
        out_shape=out_shape,
        grid=(b, s // ts),
        in_specs=in_specs,
        out_specs=out_specs,
        compiler_params=pltpu.CompilerParams(
            dimension_semantics=("arbitrary", "arbitrary"), vmem_limit_bytes=VMEM_LIMIT),
        name="b_proj_kv" if kv is not None else "b_proj",
    )(*args)


def _attn_kernel(*refs, final):
    if final:
        (q_ref, k_ref, v_ref, z_ref, h_ref, gt_ref, tab_ref, wout_ref, fg_ref,
         o_ref, kring, vring, att_ref, s_scr, p_scr, m_scr) = refs
    else:
        (q_ref, k_ref, v_ref, z_ref, h_ref, gt_ref, tab_ref, wout_ref,
         o_ref, kring, vring, att_ref, s_scr, p_scr, m_scr) = refs
    i = pl.program_id(1)

    @pl.when(i == 0)
    def _():
        kring[...] = jnp.zeros_like(kring)
        vring[:, :, :, :HEAD_DIM] = jnp.zeros(vring.shape[:3] + (HEAD_DIM,), BF16)
        vring[:, :, :, HEAD_DIM:] = jnp.ones(vring.shape[:3] + (HEAD_DIM,), BF16)

    kring[lax.rem(i, KEY_BLOCKS)] = k_ref[0]
    vring[lax.rem(i, KEY_BLOCKS), :, :, :HEAD_DIM] = v_ref[0]
    slots = [lax.rem(i + 1 + j, KEY_BLOCKS) for j in range(KEY_BLOCKS)]
    cols = [slice(j * Q_BLOCK, (j + 1) * Q_BLOCK) for j in range(KEY_BLOCKS)]

    def scores(hd, buf, masked):
        qh = q_ref[0, hd]
        keys = jnp.concatenate([kring[slots[j], hd] for j in range(KEY_BLOCKS)], axis=0)
        s = lax.dot_general(qh, keys, (((1,), (1,)), ((), ())),
                            preferred_element_type=F32)
        s = s + tab_ref[hd]
        if masked:
            s = jnp.concatenate(
                [jnp.where(i >= KEY_BLOCKS - 1 - j, s[:, cols[j]], NEG_INF)
                 for j in range(KEY_BLOCKS - 1)] + [s[:, cols[KEY_BLOCKS - 1]]], axis=1)
        s_scr[buf] = s
        m_scr[buf] = functools.reduce(
            jnp.maximum, [s[:, c * LANES:(c + 1) * LANES] for c in range(s.shape[1] // LANES)])

    def softmax(buf):
        for r in range(0, Q_BLOCK, SOFTMAX_ROWS):
            rows = slice(r, r + SOFTMAX_ROWS)
            m = m_scr[buf, rows].max(axis=-1, keepdims=True)
            p_scr[buf, rows] = jnp.exp2(s_scr[buf, rows] - m).astype(BF16)

    def values(hd, buf):
        vals = jnp.concatenate([vring[slots[j], hd] for j in range(KEY_BLOCKS)], axis=0)
        oe = jnp.dot(p_scr[buf], vals, preferred_element_type=F32)
        att_ref[hd] = oe[:, :HEAD_DIM] * (1.0 / oe[:, HEAD_DIM:])

    def all_heads(masked, unrolled):
        scores(0, 0, masked)
        scores(1, 1, masked)
        softmax(0)

        def step(it, carry):
            for u in range(2):
                hd = 1 + 2 * it + u
                scores(hd + 1, u, masked)
                softmax(1 - u)
                values(hd - 1, u)
            return carry

        if unrolled:
            for it in range((N_HEADS - 2) // 2):
                step(it, 0)
        else:
            lax.fori_loop(0, (N_HEADS - 2) // 2, step, 0)
        softmax(1)
        values(N_HEADS - 2, 0)
        values(N_HEADS - 1, 1)

    @pl.when(i >= KEY_BLOCKS - 1)
    def _():
        all_heads(masked=False, unrolled=True)

    @pl.when(i < KEY_BLOCKS - 1)
    def _():
        all_heads(masked=True, unrolled=False)

    att = jnp.concatenate([att_ref[hd] for hd in range(N_HEADS)], axis=1)
    gz = att * _silu(z_ref[0])
    y = jnp.dot(gz.astype(BF16), wout_ref[...], preferred_element_type=F32)
    out = h_ref[0] + gt_ref[0] * y
    if final:
        out = out * _rms_scale(out) * fg_ref[...]
    o_ref[0] = out


def _attn_call(q, k, v, z, h, gate, table, w_out, final_g=None):
    b, s, d = h.shape
    tq = Q_BLOCK
    heads = pl.BlockSpec((1, N_HEADS, tq, HEAD_DIM), lambda bi, i: (bi, 0, i, 0))
    row = pl.BlockSpec((1, tq, d), lambda bi, i: (bi, i, 0))
    in_specs = [heads, heads, heads,
                pl.BlockSpec((1, tq, D_INNER), lambda bi, i: (bi, i, 0)),
                row,
                pl.BlockSpec((1, 1, d), lambda bi, i: (bi, 0, 0)),
                _resident(table.shape), _resident(w_out.shape)]
    args = [q, k, v, z, h, gate, table, w_out]
    if final_g is not None:
        in_specs.append(_resident((1, d)))
        args.append(final_g)
    kring = pltpu.VMEM((KEY_BLOCKS, N_HEADS, tq, HEAD_DIM), BF16)
    vring = pltpu.VMEM((KEY_BLOCKS, N_HEADS, tq, 2 * HEAD_DIM), BF16)
    return pl.pallas_call(
        functools.partial(_attn_kernel, final=final_g is not None),
        out_shape=jax.ShapeDtypeStruct(h.shape, F32),
        grid=(b, s // tq),
        in_specs=in_specs,
        out_specs=row,
        scratch_shapes=[kring, vring, pltpu.VMEM((N_HEADS, tq, HEAD_DIM), F32),
                        pltpu.VMEM((2, tq, KEY_BLOCKS * tq), F32),
                        pltpu.VMEM((2, tq, KEY_BLOCKS * tq), BF16),
                        pltpu.VMEM((2, tq, LANES), F32)],
        compiler_params=pltpu.CompilerParams(
            dimension_semantics=("arbitrary", "arbitrary"), vmem_limit_bytes=VMEM_LIMIT),
        name="band_attn_final" if final_g is not None else "band_attn",
    )(*args)


def _bias_table(rel_bias):
    n_heads = rel_bias.shape[0]
    n_keys = KEY_BLOCKS * Q_BLOCK
    top = (KEY_BLOCKS - 1) * Q_BLOCK + Q_BLOCK - 1
    n_dist = Q_BLOCK + n_keys - 1
    vec = jnp.concatenate([
        jnp.broadcast_to(rel_bias[:, -1:], (n_heads, top - REL_CLIP)),
        rel_bias[:, ::-1],
        jnp.broadcast_to(rel_bias[:, :1], (n_heads, n_dist - (top - REL_CLIP) - rel_bias.shape[1])),
        jnp.zeros((n_heads, 1), rel_bias.dtype)], axis=1)
    skew = jnp.broadcast_to(vec[:, None, :], (n_heads, Q_BLOCK, n_dist + 1))
    skew = skew.reshape(n_heads, -1)[:, :Q_BLOCK * n_dist].reshape(n_heads, Q_BLOCK, n_dist)
    table = skew[:, :, Q_BLOCK - 1:Q_BLOCK - 1 + n_keys]
    qc = jnp.arange(Q_BLOCK)[:, None] // CHUNK
    kc = jnp.arange(n_keys)[None, :] // CHUNK
    in_band = (kc >= qc) & (kc <= qc + LEFT_CHUNKS)
    return jnp.where(in_band[None], table.astype(F32) * LOG2E, NEG_INF)


@jax.jit
def _forward(x, c, ada_w, ada_b, norm_g, a_w_in, a_w_group, a_scale, a_w_out, kv_norm_g,
             kv_ada_w, kv_ada_b, w_kv, b_w_in, b_rel_bias, b_w_out, final_g):
    b, s, d = x.shape
    depth = ada_w.shape[0]
    n_a = a_w_in.shape[0]
    c_pad = jnp.pad(c, ((0, 8 - b), (0, 0)))
    mods = _ada_call(c_pad, ada_w, ada_b.reshape(depth, 1, 3 * d))[:, :b]
    kvm = _ada_call(c_pad, kv_ada_w[None], kv_ada_b.reshape(1, 1, 2 * d))[0, :b]

    def vec(a):
        return a.reshape(b, 1, d)

    h = x
    k = v = None
    for layer in range(depth):
        shift, scale, gate = (vec(mods[layer, :, j * d:(j + 1) * d]) for j in range(3))
        g = norm_g[layer].reshape(1, d)
        if layer < n_a:
            h = _a_layer_call(h, shift, scale, gate, g,
                              a_w_in[layer].astype(BF16), a_w_group[layer].astype(BF16),
                              a_scale[layer].reshape(1, D_INNER), a_w_out[layer].astype(BF16))
        else:
            bi = layer - n_a
            w_in = b_w_in[bi].astype(BF16)
            if layer == n_a:
                kv = (vec(kvm[:, :d]), vec(kvm[:, d:]), kv_norm_g.reshape(1, d),
                      w_kv.astype(BF16))
                q, z, k, v = _bproj_call(h, shift, scale, g, w_in, kv)
            else:
                q, z = _bproj_call(h, shift, scale, g, w_in)
            last = layer == depth - 1
            h = _attn_call(q, k, v, z, h, gate, _bias_table(b_rel_bias[bi]),
                           b_w_out[bi].astype(BF16),
                           final_g.reshape(1, d) if last else None)
    return h


def kernel(x, c, ada_w, ada_b, norm_g, a_w_in, a_w_group, a_scale, a_w_out, kv_norm_g, kv_ada_w, kv_ada_b, w_kv, b_w_in, b_rel_bias, b_w_out, final_g):
    return _forward(x, c, ada_w, ada_b, norm_g, a_w_in, a_w_group, a_scale, a_w_out,
                    kv_norm_g, kv_ada_w, kv_ada_b, w_kv, b_w_in, b_rel_bias, b_w_out, final_g)
```

```python
import functools

import jax
import jax.numpy as jnp
from jax import lax
from jax.experimental import pallas as pl
from jax.experimental.pallas import tpu as pltpu

F32 = jnp.float32
BF16 = jnp.bfloat16

D_MODEL = 1024
D_INNER = 2048
N_HEADS = 16
HEAD_DIM = 128
CHUNK = 64
LEFT_CHUNKS = 8
REL_CLIP = 128
POOL_WINDOWS = (2, 4, 8, 16)
POOL_GROUP_W = D_INNER // len(POOL_WINDOWS)
POOL_HALO = 16
EPS = 1e-6
LOG2E = 1.4426950408889634
Q_PRESCALE = HEAD_DIM ** -0.5 * LOG2E
NEG_INF = float("-inf")

LANES = 128
Q_BLOCK = 256
KEY_BLOCKS = LEFT_CHUNKS * CHUNK // Q_BLOCK + 1
SOFTMAX_ROWS = 64
A_TILE = 512
P_TILE = 512
VMEM_LIMIT = 54 * 1024 * 1024


def _silu(x):
    return x * (1.0 / (1.0 + jnp.exp(-x)))


def _rms_scale(x):
    return lax.rsqrt(jnp.mean(x * x, axis=-1, keepdims=True) + EPS)


def _modulated(x, g, shift, scale):
    return ((x * _rms_scale(x)) * (g * (1.0 + scale)) + shift).astype(BF16)


def _resident(shape, layer=None):
    if layer is None:
        return pl.BlockSpec(shape, lambda *_: (0,) * len(shape), pipeline_mode=pl.Buffered(1))
    return pl.BlockSpec((None,) + tuple(shape[1:]),
                        lambda *_: (layer,) + (0,) * (len(shape) - 1),
                        pipeline_mode=pl.Buffered(1))


def _next_tile(n_b, n_s):
    def tile(bi, si):
        wrap = si == n_s - 1
        return jnp.minimum(bi + wrap, n_b - 1), jnp.where(wrap, 0, si + 1)
    return tile


def _ada_kernel(c_ref, w_ref, b_ref, o_ref):
    ca = _silu(c_ref[...])
    o_ref[0] = jnp.dot(ca, w_ref[0], preferred_element_type=F32) + b_ref[0]


def _ada_call(c_pad, w, b):
    n_layers, d, n = w.shape
    tn = 1024
    return pl.pallas_call(
        _ada_kernel,
        out_shape=jax.ShapeDtypeStruct((n_layers, c_pad.shape[0], n), F32),
        grid=(n_layers, n // tn),
        in_specs=[
            pl.BlockSpec(c_pad.shape, lambda l, j: (0, 0)),
            pl.BlockSpec((1, d, tn), lambda l, j: (l, 0, j)),
            pl.BlockSpec((1, 1, tn), lambda l, j: (l, 0, j)),
        ],
        out_specs=pl.BlockSpec((1, c_pad.shape[0], tn), lambda l, j: (l, 0, j)),
        compiler_params=pltpu.CompilerParams(
            dimension_semantics=("arbitrary", "arbitrary")),
        name="ada_mod",
    )(c_pad, w, b)


def _a_kernel(h_ref, hn_ref, shn_ref, scn_ref, gt_ref, g_ref, win_ref, wg_ref, asc_ref,
              wout_ref, o_ref, carry_ref, u_ref, un_ref):
    ts = h_ref.shape[1]
    si = pl.program_id(1)
    step = pl.program_id(0) * pl.num_programs(1) + si

    @pl.when(step == 0)
    def _():
        un_ref[...] = _modulated(h_ref[0], g_ref[...], shn_ref[0], scn_ref[0])

    u_ref[...] = un_ref[...]

    @pl.when(si == 0)
    def _():
        carry_ref[...] = jnp.zeros_like(carry_ref)

    groups = sorted(range(len(POOL_WINDOWS)), key=lambda g: -POOL_WINDOWS[g])
    span = {g: slice(g * POOL_GROUP_W, (g + 1) * POOL_GROUP_W) for g in groups}
    val = {g: jnp.dot(u_ref[...], win_ref[:, span[g]], preferred_element_type=F32)
           for g in groups}
    un_ref[...] = _modulated(hn_ref[0], g_ref[...], shn_ref[0], scn_ref[0])
    t = si * ts + lax.broadcasted_iota(jnp.int32, (ts, LANES), 0)
    gz = {}
    for g in groups:
        w = POOL_WINDOWS[g]
        z = jnp.dot(u_ref[...], win_ref[:, D_INNER + span[g].start:D_INNER + span[g].stop],
                    preferred_element_type=F32)
        s = jnp.concatenate([carry_ref[:, span[g]], val[g]], axis=0)
        carry_ref[:, span[g]] = val[g][ts - POOL_HALO:, :]
        k = 1
        while k < w:
            s = s + pltpu.roll(s, k, axis=0)
            k *= 2
        s = s[POOL_HALO:, :]
        inv = 1.0 / jnp.minimum(t + 1, w).astype(F32)
        inv = jnp.concatenate([inv] * (POOL_GROUP_W // LANES), axis=1)
        pooled = s * inv - val[g]
        mixed = jnp.dot(pooled.astype(BF16), wg_ref[g], preferred_element_type=F32)
        gz[g] = ((mixed * asc_ref[:, span[g]]) * _silu(z)).astype(BF16)
    y = jnp.dot(jnp.concatenate([gz[g] for g in sorted(groups)], axis=1), wout_ref[...],
                preferred_element_type=F32)
    o_ref[0] = h_ref[0] + gt_ref[0] * y


def _a_layer_call(h, shift, scale, gate, g, w_in, w_group, a_scale, w_out, layer):
    b, s, d = h.shape
    ts = A_TILE
    n_s = s // ts
    assert n_s > 1
    nxt = _next_tile(b, n_s)
    row = pl.BlockSpec((1, ts, d), lambda bi, si: (bi, si, 0))
    row_next = pl.BlockSpec((1, ts, d), lambda bi, si: (*nxt(bi, si), 0))
    vec = pl.BlockSpec((1, 1, d), lambda bi, si: (bi, 0, 0))
    vec_next = pl.BlockSpec((1, 1, d), lambda bi, si: (nxt(bi, si)[0], 0, 0))
    return pl.pallas_call(
        _a_kernel,
        out_shape=jax.ShapeDtypeStruct(h.shape, F32),
        grid=(b, n_s),
        in_specs=[row, row_next, vec_next, vec_next, vec,
                  _resident((1, d)), _resident(w_in.shape, layer),
                  _resident(w_group.shape, layer),
                  _resident((1, D_INNER)), _resident(w_out.shape, layer)],
        out_specs=row,
        scratch_shapes=[pltpu.VMEM((POOL_HALO, D_INNER), F32),
                        pltpu.VMEM((ts, d), BF16),
                        pltpu.VMEM((ts, d), BF16)],
        compiler_params=pltpu.CompilerParams(
            dimension_semantics=("arbitrary", "arbitrary"), vmem_limit_bytes=VMEM_LIMIT),
        name="a_layer",
    )(h, h, shift, scale, gate, g, w_in, w_group, a_scale, w_out)


def _store_heads(dst_ref, val):
    for hd in range(N_HEADS):
        dst_ref[0, hd] = val[:, hd * HEAD_DIM:(hd + 1) * HEAD_DIM].astype(BF16)


def _bproj_kernel(*refs, with_kv):
    if with_kv:
        (h_ref, sh_ref, sc_ref, g_ref, win_ref, ksh_ref, ksc_ref, kg_ref, wkv_ref,
         q_ref, z_ref, k_ref, v_ref) = refs
    else:
        h_ref, sh_ref, sc_ref, g_ref, win_ref, q_ref, z_ref = refs
    x = h_ref[0]
    xn = x * _rms_scale(x)
    u = xn * (g_ref[...] * (1.0 + sc_ref[0])) + sh_ref[0]
    qz = jnp.dot(u.astype(BF16), win_ref[...], preferred_element_type=F32)
    _store_heads(q_ref, qz[:, :D_INNER] * Q_PRESCALE)
    z_ref[0] = qz[:, D_INNER:]
    if with_kv:
        hk = xn * (kg_ref[...] * (1.0 + ksc_ref[0])) + ksh_ref[0]
        kv = jnp.dot(hk.astype(BF16), wkv_ref[...], preferred_element_type=F32)
        _store_heads(k_ref, kv[:, :D_INNER])
        _store_heads(v_ref, kv[:, D_INNER:])


def _bproj_call(h, shift, scale, g, w_in, layer, kv=None):
    b, s, d = h.shape
    ts = P_TILE
    row = pl.BlockSpec((1, ts, d), lambda bi, si: (bi, si, 0))
    vec = pl.BlockSpec((1, 1, d), lambda bi, si: (bi, 0, 0))
    heads = pl.BlockSpec((1, N_HEADS, ts, HEAD_DIM), lambda bi, si: (bi, 0, si, 0))
    heads_shape = jax.ShapeDtypeStruct((b, N_HEADS, s, HEAD_DIM), BF16)
    in_specs = [row, vec, vec, _resident((1, d)), _resident(w_in.shape, layer)]
    args = [h, shift, scale, g, w_in]
    out_shape = [heads_shape, jax.ShapeDtypeStruct((b, s, D_INNER), F32)]
    out_specs = [heads, pl.BlockSpec((1, ts, D_INNER), lambda bi, si: (bi, si, 0))]
    if kv is not None:
        kshift, kscale, kg, w_kv = kv
        in_specs += [vec, vec, _resident((1, d)), _resident(w_kv.shape)]
        args += [kshift, kscale, kg, w_kv]
        out_shape += [heads_shape, heads_shape]
        out_specs += [heads, heads]
    return pl.pallas_call(
        functools.partial(_bproj_kernel, with_kv=kv is not None),
        out_shape=out_shape,
        grid=(b, s // ts),
        in_specs=in_specs,
        out_specs=out_specs,
        compiler_params=pltpu.CompilerParams(
            dimension_semantics=("arbitrary", "arbitrary"), vmem_limit_bytes=VMEM_LIMIT),
        name="b_proj_kv" if kv is not None else "b_proj",
    )(*args)


def _attn_kernel(*refs, final):
    if final:
        (q_ref, k_ref, v_ref, z_ref, h_ref, gt_ref, bvec_ref, wout_ref, fg_ref,
         o_ref, kring, vring, att_ref, s_scr, p_scr, m_scr, tab_ref) = refs
    else:
        (q_ref, k_ref, v_ref, z_ref, h_ref, gt_ref, bvec_ref, wout_ref,
         o_ref, kring, vring, att_ref, s_scr, p_scr, m_scr, tab_ref) = refs
    i = pl.program_id(1)
    n_keys = KEY_BLOCKS * Q_BLOCK

    @pl.when((pl.program_id(0) == 0) & (i == 0))
    def _():
        def one_head(hd, carry):
            skew = pltpu.roll(jnp.broadcast_to(bvec_ref[hd], (Q_BLOCK, bvec_ref.shape[-1])),
                              0, 1, stride=1, stride_axis=0)
            qc = lax.broadcasted_iota(jnp.int32, (Q_BLOCK, n_keys), 0) // CHUNK
            kc = lax.broadcasted_iota(jnp.int32, (Q_BLOCK, n_keys), 1) // CHUNK
            bias = skew[:, :n_keys] * LOG2E
            bias = jnp.where(kc >= qc, bias, NEG_INF)
            tab_ref[hd] = jnp.where(kc <= qc + LEFT_CHUNKS, bias, NEG_INF)
            return carry

        lax.fori_loop(0, N_HEADS, one_head, 0)

    @pl.when(i == 0)
    def _():
        kring[...] = jnp.zeros_like(kring)
        vring[:, :, :, :HEAD_DIM] = jnp.zeros(vring.shape[:3] + (HEAD_DIM,), BF16)
        vring[:, :, :, HEAD_DIM:] = jnp.ones(vring.shape[:3] + (HEAD_DIM,), BF16)

    kring[lax.rem(i, KEY_BLOCKS)] = k_ref[0]
    vring[lax.rem(i, KEY_BLOCKS), :, :, :HEAD_DIM] = v_ref[0]
    slots = [lax.rem(i + 1 + j, KEY_BLOCKS) for j in range(KEY_BLOCKS)]
    cols = [slice(j * Q_BLOCK, (j + 1) * Q_BLOCK) for j in range(KEY_BLOCKS)]

    def scores(hd, buf, masked):
        qh = q_ref[0, hd]
        keys = jnp.concatenate([kring[slots[j], hd] for j in range(KEY_BLOCKS)], axis=0)
        s = lax.dot_general(qh, keys, (((1,), (1,)), ((), ())),
                            preferred_element_type=F32)
        s = s + tab_ref[hd]
        if masked:
            s = jnp.concatenate(
                [jnp.where(i >= KEY_BLOCKS - 1 - j, s[:, cols[j]], NEG_INF)
                 for j in range(KEY_BLOCKS - 1)] + [s[:, cols[KEY_BLOCKS - 1]]], axis=1)
        s_scr[buf] = s
        m_scr[buf] = functools.reduce(
            jnp.maximum, [s[:, c * LANES:(c + 1) * LANES] for c in range(s.shape[1] // LANES)])

    def softmax(buf):
        for r in range(0, Q_BLOCK, SOFTMAX_ROWS):
            rows = slice(r, r + SOFTMAX_ROWS)
            m = m_scr[buf, rows].max(axis=-1, keepdims=True)
            p_scr[buf, rows] = jnp.exp2(s_scr[buf, rows] - m).astype(BF16)

    def values(hd, buf):
        vals = jnp.concatenate([vring[slots[j], hd] for j in range(KEY_BLOCKS)], axis=0)
        oe = jnp.dot(p_scr[buf], vals, preferred_element_type=F32)
        att_ref[hd] = oe[:, :HEAD_DIM] * (1.0 / oe[:, HEAD_DIM:])

    def all_heads(masked, unrolled):
        scores(0, 0, masked)
        scores(1, 1, masked)
        softmax(0)

        def step(it, carry):
            for u in range(2):
                hd = 1 + 2 * it + u
                scores(hd + 1, u, masked)
                softmax(1 - u)
                values(hd - 1, u)
            return carry

        if unrolled:
            for it in range((N_HEADS - 2) // 2):
                step(it, 0)
        else:
            lax.fori_loop(0, (N_HEADS - 2) // 2, step, 0)
        softmax(1)
        values(N_HEADS - 2, 0)
        values(N_HEADS - 1, 1)

    @pl.when(i >= KEY_BLOCKS - 1)
    def _():
        all_heads(masked=False, unrolled=True)

    @pl.when(i < KEY_BLOCKS - 1)
    def _():
        all_heads(masked=True, unrolled=False)

    att = jnp.concatenate([att_ref[hd] for hd in range(N_HEADS)], axis=1)
    gz = att * _silu(z_ref[0])
    y = jnp.dot(gz.astype(BF16), wout_ref[...], preferred_element_type=F32)
    out = h_ref[0] + gt_ref[0] * y
    if final:
        out = out * _rms_scale(out) * fg_ref[...]
    o_ref[0] = out


def _attn_call(q, k, v, z, h, gate, bias_vec, w_out, layer, final_g=None):
    b, s, d = h.shape
    tq = Q_BLOCK
    heads = pl.BlockSpec((1, N_HEADS, tq, HEAD_DIM), lambda bi, i: (bi, 0, i, 0))
    row = pl.BlockSpec((1, tq, d), lambda bi, i: (bi, i, 0))
    in_specs = [heads, heads, heads,
                pl.BlockSpec((1, tq, D_INNER), lambda bi, i: (bi, i, 0)),
                row,
                pl.BlockSpec((1, 1, d), lambda bi, i: (bi, 0, 0)),
                _resident(bias_vec.shape), _resident(w_out.shape, layer)]
    args = [q, k, v, z, h, gate, bias_vec, w_out]
    if final_g is not None:
        in_specs.append(_resident((1, d)))
        args.append(final_g)
    kring = pltpu.VMEM((KEY_BLOCKS, N_HEADS, tq, HEAD_DIM), BF16)
    vring = pltpu.VMEM((KEY_BLOCKS, N_HEADS, tq, 2 * HEAD_DIM), BF16)
    return pl.pallas_call(
        functools.partial(_attn_kernel, final=final_g is not None),
        out_shape=jax.ShapeDtypeStruct(h.shape, F32),
        grid=(b, s // tq),
        in_specs=in_specs,
        out_specs=row,
        scratch_shapes=[kring, vring, pltpu.VMEM((N_HEADS, tq, HEAD_DIM), F32),
                        pltpu.VMEM((2, tq, KEY_BLOCKS * tq), F32),
                        pltpu.VMEM((2, tq, KEY_BLOCKS * tq), BF16),
                        pltpu.VMEM((2, tq, LANES), F32),
                        pltpu.VMEM((N_HEADS, tq, KEY_BLOCKS * tq), F32)],
        compiler_params=pltpu.CompilerParams(
            dimension_semantics=("arbitrary", "arbitrary"), vmem_limit_bytes=VMEM_LIMIT),
        name="band_attn_final" if final_g is not None else "band_attn",
    )(*args)


def _bias_vector(rel_bias):
    n_heads = rel_bias.shape[0]
    n_keys = KEY_BLOCKS * Q_BLOCK
    top = (KEY_BLOCKS - 1) * Q_BLOCK + Q_BLOCK - 1
    n_dist = Q_BLOCK + n_keys - 1
    vec = jnp.concatenate([
        jnp.broadcast_to(rel_bias[:, -1:], (n_heads, top - REL_CLIP)),
        rel_bias[:, ::-1],
        jnp.broadcast_to(rel_bias[:, :1], (n_heads, n_dist - (top - REL_CLIP) - rel_bias.shape[1])),
        jnp.zeros((n_heads, 1), rel_bias.dtype)], axis=1)
    return jnp.roll(vec, -(Q_BLOCK - 1), axis=1).astype(F32)[:, None, :]


@jax.jit
def _forward(x, c, ada_w, ada_b, norm_g, a_w_in, a_w_group, a_scale, a_w_out, kv_norm_g,
             kv_ada_w, kv_ada_b, w_kv, b_w_in, b_rel_bias, b_w_out, final_g):
    b, s, d = x.shape
    depth = ada_w.shape[0]
    n_a = a_w_in.shape[0]
    c_pad = jnp.pad(c, ((0, 8 - b), (0, 0)))
    mods = _ada_call(c_pad, ada_w, ada_b.reshape(depth, 1, 3 * d))[:, :b]
    kvm = _ada_call(c_pad, kv_ada_w[None], kv_ada_b.reshape(1, 1, 2 * d))[0, :b]

    def vec(a):
        return a.reshape(b, 1, d)

    a_w_in, a_w_group, a_w_out = (w.astype(BF16) for w in (a_w_in, a_w_group, a_w_out))
    b_w_in, b_w_out, w_kv = (w.astype(BF16) for w in (b_w_in, b_w_out, w_kv))

    h = x
    k = v = None
    for layer in range(depth):
        shift, scale, gate = (vec(mods[layer, :, j * d:(j + 1) * d]) for j in range(3))
        g = norm_g[layer].reshape(1, d)
        if layer < n_a:
            h = _a_layer_call(h, shift, scale, gate, g, a_w_in, a_w_group,
                              a_scale[layer].reshape(1, D_INNER), a_w_out, layer)
        else:
            bi = layer - n_a
            if layer == n_a:
                kv = (vec(kvm[:, :d]), vec(kvm[:, d:]), kv_norm_g.reshape(1, d), w_kv)
                q, z, k, v = _bproj_call(h, shift, scale, g, b_w_in, bi, kv)
            else:
                q, z = _bproj_call(h, shift, scale, g, b_w_in, bi)
            last = layer == depth - 1
            h = _attn_call(q, k, v, z, h, gate, _bias_vector(b_rel_bias[bi]), b_w_out, bi,
                           final_g.reshape(1, d) if last else None)
    return h


def kernel(x, c, ada_w, ada_b, norm_g, a_w_in, a_w_group, a_scale, a_w_out, kv_norm_g, kv_ada_w, kv_ada_b, w_kv, b_w_in, b_rel_bias, b_w_out, final_g):
    return _forward(x, c, ada_w, ada_b, norm_g, a_w_in, a_w_group, a_scale, a_w_out,
                    kv_norm_g, kv_ada_w, kv_ada_b, w_kv, b_w_in, b_rel_bias, b_w_out, final_g)
```

```python
import functools

import jax
import jax.numpy as jnp
from jax import lax
from jax.experimental import pallas as pl
from jax.experimental.pallas import tpu as pltpu

F32 = jnp.float32
BF16 = jnp.bfloat16

D_MODEL = 1024
D_INNER = 2048
N_HEADS = 16
HEAD_DIM = 128
CHUNK = 64
LEFT_CHUNKS = 8
REL_CLIP = 128
POOL_WINDOWS = (2, 4, 8, 16)
POOL_GROUP_W = D_INNER // len(POOL_WINDOWS)
POOL_HALO = 16
EPS = 1e-6
LOG2E = 1.4426950408889634
Q_PRESCALE = HEAD_DIM ** -0.5 * LOG2E
NEG_INF = float("-inf")

LANES = 128
Q_BLOCK = 256
KEY_BLOCKS = LEFT_CHUNKS * CHUNK // Q_BLOCK + 1
SOFTMAX_ROWS = 64
A_TILE = 512
P_TILE = 512
VMEM_LIMIT = 54 * 1024 * 1024


def _silu(x):
    return x * (1.0 / (1.0 + jnp.exp(-x)))


def _rms_scale(x):
    return lax.rsqrt(jnp.mean(x * x, axis=-1, keepdims=True) + EPS)


def _modulated(x, g, shift, scale):
    return ((x * _rms_scale(x)) * (g * (1.0 + scale)) + shift).astype(BF16)


def _resident(shape, layer=None):
    if layer is None:
        return pl.BlockSpec(shape, lambda *_: (0,) * len(shape), pipeline_mode=pl.Buffered(1))
    return pl.BlockSpec((None,) + tuple(shape[1:]),
                        lambda *_: (layer,) + (0,) * (len(shape) - 1),
                        pipeline_mode=pl.Buffered(1))


def _next_tile(n_b, n_s):
    def tile(bi, si):
        wrap = si == n_s - 1
        return jnp.minimum(bi + wrap, n_b - 1), jnp.where(wrap, 0, si + 1)
    return tile


def _ada_kernel(c_ref, w_ref, b_ref, o_ref):
    ca = _silu(c_ref[...])
    o_ref[0] = jnp.dot(ca, w_ref[0], preferred_element_type=F32) + b_ref[0]


def _ada_call(c_pad, w, b):
    n_layers, d, n = w.shape
    tn = 1024
    return pl.pallas_call(
        _ada_kernel,
        out_shape=jax.ShapeDtypeStruct((n_layers, c_pad.shape[0], n), F32),
        grid=(n_layers, n // tn),
        in_specs=[
            pl.BlockSpec(c_pad.shape, lambda l, j: (0, 0)),
            pl.BlockSpec((1, d, tn), lambda l, j: (l, 0, j)),
            pl.BlockSpec((1, 1, tn), lambda l, j: (l, 0, j)),
        ],
        out_specs=pl.BlockSpec((1, c_pad.shape[0], tn), lambda l, j: (l, 0, j)),
        compiler_params=pltpu.CompilerParams(
            dimension_semantics=("arbitrary", "arbitrary")),
        name="ada_mod",
    )(c_pad, w, b)


def _a_kernel(h_ref, hn_ref, shn_ref, scn_ref, gt_ref, g_ref, win_ref, wg_ref, asc_ref,
              wout_ref, o_ref, carry_ref, u_ref, un_ref):
    ts = h_ref.shape[1]
    si = pl.program_id(1)
    step = pl.program_id(0) * pl.num_programs(1) + si

    @pl.when(step == 0)
    def _():
        un_ref[...] = _modulated(h_ref[0], g_ref[...], shn_ref[0], scn_ref[0])

    u_ref[...] = un_ref[...]

    @pl.when(si == 0)
    def _():
        carry_ref[...] = jnp.zeros_like(carry_ref)

    groups = sorted(range(len(POOL_WINDOWS)), key=lambda g: -POOL_WINDOWS[g])
    span = {g: slice(g * POOL_GROUP_W, (g + 1) * POOL_GROUP_W) for g in groups}
    val = {g: jnp.dot(u_ref[...], win_ref[:, span[g]], preferred_element_type=F32)
           for g in groups}
    un_ref[...] = _modulated(hn_ref[0], g_ref[...], shn_ref[0], scn_ref[0])
    t = si * ts + lax.broadcasted_iota(jnp.int32, (ts, LANES), 0)
    gz = {}
    for g in groups:
        w = POOL_WINDOWS[g]
        z = jnp.dot(u_ref[...], win_ref[:, D_INNER + span[g].start:D_INNER + span[g].stop],
                    preferred_element_type=F32)
        s = jnp.concatenate([carry_ref[:, span[g]], val[g]], axis=0)
        carry_ref[:, span[g]] = val[g][ts - POOL_HALO:, :]
        k = 1
        while k < w:
            s = s + pltpu.roll(s, k, axis=0)
            k *= 2
        s = s[POOL_HALO:, :]
        inv = 1.0 / jnp.minimum(t + 1, w).astype(F32)
        inv = jnp.concatenate([inv] * (POOL_GROUP_W // LANES), axis=1)
        pooled = s * inv - val[g]
        mixed = jnp.dot(pooled.astype(BF16), wg_ref[g], preferred_element_type=F32)
        gz[g] = ((mixed * asc_ref[:, span[g]]) * _silu(z)).astype(BF16)
    y = jnp.dot(jnp.concatenate([gz[g] for g in sorted(groups)], axis=1), wout_ref[...],
                preferred_element_type=F32)
    o_ref[0] = h_ref[0] + gt_ref[0] * y


def _a_layer_call(h, shift, scale, gate, g, w_in, w_group, a_scale, w_out, layer):
    b, s, d = h.shape
    ts = A_TILE
    n_s = s // ts
    assert n_s > 1
    nxt = _next_tile(b, n_s)
    row = pl.BlockSpec((1, ts, d), lambda bi, si: (bi, si, 0))
    row_next = pl.BlockSpec((1, ts, d), lambda bi, si: (*nxt(bi, si), 0))
    vec = pl.BlockSpec((1, 1, d), lambda bi, si: (bi, 0, 0))
    vec_next = pl.BlockSpec((1, 1, d), lambda bi, si: (nxt(bi, si)[0], 0, 0))
    return pl.pallas_call(
        _a_kernel,
        out_shape=jax.ShapeDtypeStruct(h.shape, F32),
        grid=(b, n_s),
        in_specs=[row, row_next, vec_next, vec_next, vec,
                  _resident((1, d)), _resident(w_in.shape, layer),
                  _resident(w_group.shape, layer),
                  _resident((1, D_INNER)), _resident(w_out.shape, layer)],
        out_specs=row,
        scratch_shapes=[pltpu.VMEM((POOL_HALO, D_INNER), F32),
                        pltpu.VMEM((ts, d), BF16),
                        pltpu.VMEM((ts, d), BF16)],
        compiler_params=pltpu.CompilerParams(
            dimension_semantics=("arbitrary", "arbitrary"), vmem_limit_bytes=VMEM_LIMIT),
        name="a_layer",
    )(h, h, shift, scale, gate, g, w_in, w_group, a_scale, w_out)


def _store_heads(dst_ref, val):
    for hd in range(N_HEADS):
        dst_ref[0, hd] = val[:, hd * HEAD_DIM:(hd + 1) * HEAD_DIM].astype(BF16)


def _bproj_kernel(*refs, with_kv):
    if with_kv:
        (h_ref, sh_ref, sc_ref, g_ref, win_ref, ksh_ref, ksc_ref, kg_ref, wkv_ref,
         q_ref, z_ref, k_ref, v_ref) = refs
    else:
        h_ref, sh_ref, sc_ref, g_ref, win_ref, q_ref, z_ref = refs
    x = h_ref[0]
    xn = x * _rms_scale(x)
    u = xn * (g_ref[...] * (1.0 + sc_ref[0])) + sh_ref[0]
    qz = jnp.dot(u.astype(BF16), win_ref[...], preferred_element_type=F32)
    _store_heads(q_ref, qz[:, :D_INNER] * Q_PRESCALE)
    z_ref[0] = qz[:, D_INNER:]
    if with_kv:
        hk = xn * (kg_ref[...] * (1.0 + ksc_ref[0])) + ksh_ref[0]
        kv = jnp.dot(hk.astype(BF16), wkv_ref[...], preferred_element_type=F32)
        _store_heads(k_ref, kv[:, :D_INNER])
        _store_heads(v_ref, kv[:, D_INNER:])


def _bproj_call(h, shift, scale, g, w_in, layer, kv=None):
    b, s, d = h.shape
    ts = P_TILE
    row = pl.BlockSpec((1, ts, d), lambda bi, si: (bi, si, 0))
    vec = pl.BlockSpec((1, 1, d), lambda bi, si: (bi, 0, 0))
    heads = pl.BlockSpec((1, N_HEADS, ts, HEAD_DIM), lambda bi, si: (bi, 0, si, 0))
    heads_shape = jax.ShapeDtypeStruct((b, N_HEADS, s, HEAD_DIM), BF16)
    in_specs = [row, vec, vec, _resident((1, d)), _resident(w_in.shape, layer)]
    args = [h, shift, scale, g, w_in]
    out_shape = [heads_shape, jax.ShapeDtypeStruct((b, s, D_INNER), F32)]
    out_specs = [heads, pl.BlockSpec((1, ts, D_INNER), lambda bi, si: (bi, si, 0))]
    if kv is not None:
        kshift, kscale, kg, w_kv = kv
        in_specs += [vec, vec, _resident((1, d)), _resident(w_kv.shape)]
        args += [kshift, kscale, kg, w_kv]
        out_shape += [heads_shape, heads_shape]
        out_specs += [heads, heads]
    return pl.pallas_call(
        functools.partial(_bproj_kernel, with_kv=kv is not None),
        out_shape=out_shape,
        grid=(b, s // ts),
        in_specs=in_specs,
        out_specs=out_specs,
        compiler_params=pltpu.CompilerParams(
            dimension_semantics=("arbitrary", "arbitrary"), vmem_limit_bytes=VMEM_LIMIT),
        name="b_proj_kv" if kv is not None else "b_proj",
    )(*args)


def _attn_kernel(*refs, final):
    if final:
        (q_ref, k_ref, v_ref, z_ref, h_ref, gt_ref, bvec_ref, wout_ref, fg_ref,
         o_ref, kring, vring, att_ref, s_scr, p_scr, m_scr, tab_ref) = refs
    else:
        (q_ref, k_ref, v_ref, z_ref, h_ref, gt_ref, bvec_ref, wout_ref,
         o_ref, kring, vring, att_ref, s_scr, p_scr, m_scr, tab_ref) = refs
    i = pl.program_id(1)
    n_keys = KEY_BLOCKS * Q_BLOCK

    @pl.when((pl.program_id(0) == 0) & (i == 0))
    def _():
        def one_head(hd, carry):
            skew = pltpu.roll(jnp.broadcast_to(bvec_ref[hd], (Q_BLOCK, bvec_ref.shape[-1])),
                              0, 1, stride=1, stride_axis=0)
            qc = lax.broadcasted_iota(jnp.int32, (Q_BLOCK, n_keys), 0) // CHUNK
            kc = lax.broadcasted_iota(jnp.int32, (Q_BLOCK, n_keys), 1) // CHUNK
            bias = skew[:, :n_keys] * LOG2E
            bias = jnp.where(kc >= qc, bias, NEG_INF)
            tab_ref[hd] = jnp.where(kc <= qc + LEFT_CHUNKS, bias, NEG_INF)
            return carry

        lax.fori_loop(0, N_HEADS, one_head, 0)

    @pl.when(i == 0)
    def _():
        kring[...] = jnp.zeros_like(kring)
        vring[:, :, :, :HEAD_DIM] = jnp.zeros(vring.shape[:3] + (HEAD_DIM,), BF16)
        vring[:, :, :, HEAD_DIM:] = jnp.ones(vring.shape[:3] + (HEAD_DIM,), BF16)

    slots = [lax.rem(i + 1 + j, KEY_BLOCKS) for j in range(KEY_BLOCKS - 1)]
    cols = [slice(j * Q_BLOCK, (j + 1) * Q_BLOCK) for j in range(KEY_BLOCKS)]
    ones = jnp.ones((Q_BLOCK, HEAD_DIM), BF16)

    def scores(hd, buf, masked):
        qh = q_ref[0, hd]
        keys = jnp.concatenate([kring[s, hd] for s in slots] + [k_ref[0, hd]], axis=0)
        s = lax.dot_general(qh, keys, (((1,), (1,)), ((), ())),
                            preferred_element_type=F32)
        s = s + tab_ref[hd]
        if masked:
            s = jnp.concatenate(
                [jnp.where(i >= KEY_BLOCKS - 1 - j, s[:, cols[j]], NEG_INF)
                 for j in range(KEY_BLOCKS - 1)] + [s[:, cols[KEY_BLOCKS - 1]]], axis=1)
        s_scr[buf] = s
        m_scr[buf] = functools.reduce(
            jnp.maximum, [s[:, c * LANES:(c + 1) * LANES] for c in range(s.shape[1] // LANES)])

    def softmax(buf):
        for r in range(0, Q_BLOCK, SOFTMAX_ROWS):
            rows = slice(r, r + SOFTMAX_ROWS)
            m = m_scr[buf, rows].max(axis=-1, keepdims=True)
            p_scr[buf, rows] = jnp.exp2(s_scr[buf, rows] - m).astype(BF16)

    def values(hd, buf):
        own = jnp.concatenate([v_ref[0, hd], ones], axis=1)
        vals = jnp.concatenate([vring[s, hd] for s in slots] + [own], axis=0)
        oe = jnp.dot(p_scr[buf], vals, preferred_element_type=F32)
        att_ref[hd] = oe[:, :HEAD_DIM] * (1.0 / oe[:, HEAD_DIM:])

    def finish():
        kring[lax.rem(i, KEY_BLOCKS)] = k_ref[0]
        vring[lax.rem(i, KEY_BLOCKS), :, :, :HEAD_DIM] = v_ref[0]
        att =jnp.concatenate([att_ref[hd] for hd in range(N_HEADS)], axis=1)
        gz = att * _silu(z_ref[0])
        y = jnp.dot(gz.astype(BF16), wout_ref[...], preferred_element_type=F32)
        out = h_ref[0] + gt_ref[0] * y
        if final:
            out = out * _rms_scale(out) * fg_ref[...]
        o_ref[0] = out

    def all_heads(masked, unrolled):
        scores(0, 0, masked)
        scores(1, 1, masked)
        softmax(0)

        def step(it, carry):
            for u in range(2):
                hd = 1 + 2 * it + u
                scores(hd + 1, u, masked)
                softmax(1 - u)
                values(hd - 1, u)
            return carry

        if unrolled:
            for it in range((N_HEADS - 2) // 2):
                step(it, 0)
        else:
            lax.fori_loop(0, (N_HEADS - 2) // 2, step, 0)
        softmax(1)
        values(N_HEADS - 2, 0)
        values(N_HEADS - 1, 1)
        finish()

    @pl.when(i >= KEY_BLOCKS - 1)
    def _():
        all_heads(masked=False, unrolled=True)

    @pl.when(i < KEY_BLOCKS - 1)
    def _():
        all_heads(masked=True, unrolled=False)


def _attn_call(q, k, v, z, h, gate, bias_vec, w_out, layer, final_g=None):
    b, s, d = h.shape
    tq = Q_BLOCK
    heads = pl.BlockSpec((1, N_HEADS, tq, HEAD_DIM), lambda bi, i: (bi, 0, i, 0))
    row = pl.BlockSpec((1, tq, d), lambda bi, i: (bi, i, 0))
    in_specs = [heads, heads, heads,
                pl.BlockSpec((1, tq, D_INNER), lambda bi, i: (bi, i, 0)),
                row,
                pl.BlockSpec((1, 1, d), lambda bi, i: (bi, 0, 0)),
                _resident(bias_vec.shape), _resident(w_out.shape, layer)]
    args = [q, k, v, z, h, gate, bias_vec, w_out]
    if final_g is not None:
        in_specs.append(_resident((1, d)))
        args.append(final_g)
    kring = pltpu.VMEM((KEY_BLOCKS, N_HEADS, tq, HEAD_DIM), BF16)
    vring = pltpu.VMEM((KEY_BLOCKS, N_HEADS, tq, 2 * HEAD_DIM), BF16)
    return pl.pallas_call(
        functools.partial(_attn_kernel, final=final_g is not None),
        out_shape=jax.ShapeDtypeStruct(h.shape, F32),
        grid=(b, s // tq),
        in_specs=in_specs,
        out_specs=row,
        scratch_shapes=[kring, vring, pltpu.VMEM((N_HEADS, tq, HEAD_DIM), F32),
                        pltpu.VMEM((2, tq, KEY_BLOCKS * tq), F32),
                        pltpu.VMEM((2, tq, KEY_BLOCKS * tq), BF16),
                        pltpu.VMEM((2, tq, LANES), F32),
                        pltpu.VMEM((N_HEADS, tq, KEY_BLOCKS * tq), F32)],
        compiler_params=pltpu.CompilerParams(
            dimension_semantics=("arbitrary", "arbitrary"), vmem_limit_bytes=VMEM_LIMIT),
        name="band_attn_final" if final_g is not None else "band_attn",
    )(*args)


def _bias_vector(rel_bias):
    n_heads = rel_bias.shape[0]
    n_keys = KEY_BLOCKS * Q_BLOCK
    top = (KEY_BLOCKS - 1) * Q_BLOCK + Q_BLOCK - 1
    n_dist = Q_BLOCK + n_keys - 1
    vec = jnp.concatenate([
        jnp.broadcast_to(rel_bias[:, -1:], (n_heads, top - REL_CLIP)),
        rel_bias[:, ::-1],
        jnp.broadcast_to(rel_bias[:, :1], (n_heads, n_dist - (top - REL_CLIP) - rel_bias.shape[1])),
        jnp.zeros((n_heads, 1), rel_bias.dtype)], axis=1)
    return jnp.roll(vec, -(Q_BLOCK - 1), axis=1).astype(F32)[:, None, :]


@jax.jit
def _forward(x, c, ada_w, ada_b, norm_g, a_w_in, a_w_group, a_scale, a_w_out, kv_norm_g,
             kv_ada_w, kv_ada_b, w_kv, b_w_in, b_rel_bias, b_w_out, final_g):
    b, s, d = x.shape
    depth = ada_w.shape[0]
    n_a = a_w_in.shape[0]
    c_pad = jnp.pad(c, ((0, 8 - b), (0, 0)))
    mods = _ada_call(c_pad, ada_w, ada_b.reshape(depth, 1, 3 * d))[:, :b]
    kvm = _ada_call(c_pad, kv_ada_w[None], kv_ada_b.reshape(1, 1, 2 * d))[0, :b]

    def vec(a):
        return a.reshape(b, 1, d)

    a_w_in, a_w_group, a_w_out = (w.astype(BF16) for w in (a_w_in, a_w_group, a_w_out))
    b_w_in, b_w_out, w_kv = (w.astype(BF16) for w in (b_w_in, b_w_out, w_kv))

    h = x
    k = v = None
    for layer in range(depth):
        shift, scale, gate = (vec(mods[layer, :, j * d:(j + 1) * d]) for j in range(3))
        g = norm_g[layer].reshape(1, d)
        if layer < n_a:
            h = _a_layer_call(h, shift, scale, gate, g, a_w_in, a_w_group,
                              a_scale[layer].reshape(1, D_INNER), a_w_out, layer)
        else:
            bi = layer - n_a
            if layer == n_a:
                kv = (vec(kvm[:, :d]), vec(kvm[:, d:]), kv_norm_g.reshape(1, d), w_kv)
                q, z, k, v = _bproj_call(h, shift, scale, g, b_w_in, bi, kv)
            else:
                q, z = _bproj_call(h, shift, scale, g, b_w_in, bi)
            last = layer == depth - 1
            h = _attn_call(q, k, v, z, h, gate, _bias_vector(b_rel_bias[bi]), b_w_out, bi,
                           final_g.reshape(1, d) if last else None)
    return h


def kernel(x, c, ada_w, ada_b, norm_g, a_w_in, a_w_group, a_scale, a_w_out, kv_norm_g, kv_ada_w, kv_ada_b, w_kv, b_w_in, b_rel_bias, b_w_out, final_g):
    return _forward(x, c, ada_w, ada_b, norm_g, a_w_in, a_w_group, a_scale, a_w_out,
                    kv_norm_g, kv_ada_w, kv_ada_b, w_kv, b_w_in, b_rel_bias, b_w_out, final_g)
```

```python
import functools

import jax
import jax.numpy as jnp
from jax import lax
from jax.experimental import pallas as pl
from jax.experimental.pallas import tpu as pltpu

F32 = jnp.float32
BF16 = jnp.bfloat16

D_MODEL = 1024
D_INNER = 2048
N_HEADS = 16
HEAD_DIM = 128
CHUNK = 64
LEFT_CHUNKS = 8
REL_CLIP = 128
POOL_WINDOWS = (2, 4, 8, 16)
POOL_GROUP_W = D_INNER // len(POOL_WINDOWS)
POOL_HALO = 16
EPS = 1e-6
LOG2E = 1.4426950408889634
Q_PRESCALE = HEAD_DIM ** -0.5 * LOG2E
NEG_INF = float("-inf")

LANES = 128
Q_BLOCK = 256
KEY_BLOCKS = LEFT_CHUNKS * CHUNK // Q_BLOCK + 1
SOFTMAX_ROWS = 64
A_TILE = 512
P_TILE = 512
VMEM_LIMIT = 54 * 1024 * 1024


def _silu(x):
    return x * (1.0 / (1.0 + jnp.exp(-x)))


def _rms_scale(x):
    return lax.rsqrt(jnp.mean(x * x, axis=-1, keepdims=True) + EPS)


def _modulated(x, g, shift, scale):
    return ((x * _rms_scale(x)) * (g * (1.0 + scale)) + shift).astype(BF16)


def _resident(shape, layer=None):
    if layer is None:
        return pl.BlockSpec(shape, lambda *_: (0,) * len(shape), pipeline_mode=pl.Buffered(1))
    return pl.BlockSpec((None,) + tuple(shape[1:]),
                        lambda *_: (layer,) + (0,) * (len(shape) - 1),
                        pipeline_mode=pl.Buffered(1))


def _next_tile(n_b, n_s):
    def tile(bi, si):
        wrap = si == n_s - 1
        return jnp.minimum(bi + wrap, n_b - 1), jnp.where(wrap, 0, si + 1)
    return tile


def _ada_kernel(c_ref, w_ref, b_ref, o_ref):
    ca = _silu(c_ref[...])
    o_ref[0] = jnp.dot(ca, w_ref[0], preferred_element_type=F32) + b_ref[0]


def _ada_call(c_pad, w, b):
    n_layers, d, n = w.shape
    tn = 1024
    return pl.pallas_call(
        _ada_kernel,
        out_shape=jax.ShapeDtypeStruct((n_layers, c_pad.shape[0], n), F32),
        grid=(n_layers, n // tn),
        in_specs=[
            pl.BlockSpec(c_pad.shape, lambda l, j: (0, 0)),
            pl.BlockSpec((1, d, tn), lambda l, j: (l, 0, j)),
            pl.BlockSpec((1, 1, tn), lambda l, j: (l, 0, j)),
        ],
        out_specs=pl.BlockSpec((1, c_pad.shape[0], tn), lambda l, j: (l, 0, j)),
        compiler_params=pltpu.CompilerParams(
            dimension_semantics=("arbitrary", "arbitrary")),
        name="ada_mod",
    )(c_pad, w, b)


def _fold_kernel(win_ref, wg_ref, o_ref, *, n_groups):
    j = pl.program_id(1)

    @pl.when(j < n_groups)
    def _():
        o_ref[0] = jnp.dot(win_ref[0], wg_ref[0, 0], precision=lax.Precision.HIGHEST,
                           preferred_element_type=F32).astype(BF16)

    @pl.when(j >= n_groups)
    def _():
        o_ref[0] = win_ref[0].astype(BF16)


def _fold_call(w_in, w_group):
    n_layers, d, n = w_in.shape
    n_groups, gw = w_group.shape[1], w_group.shape[2]
    return pl.pallas_call(
        functools.partial(_fold_kernel, n_groups=n_groups),
        out_shape=jax.ShapeDtypeStruct(w_in.shape, BF16),
        grid=(n_layers, n // gw),
        in_specs=[pl.BlockSpec((1, d, gw), lambda l, j: (l, 0, j)),
                  pl.BlockSpec((1, 1, gw, gw), lambda l, j: (l, jnp.minimum(j, n_groups - 1), 0, 0))],
        out_specs=pl.BlockSpec((1, d, gw), lambda l, j: (l, 0, j)),
        compiler_params=pltpu.CompilerParams(
            dimension_semantics=("arbitrary", "arbitrary")),
        name="fold_group_mix",
    )(w_in, w_group)


def _a_kernel(h_ref, hn_ref, shn_ref, scn_ref, gt_ref, g_ref, win_ref, asc_ref,
              wout_ref, o_ref, carry_ref, u_ref, un_ref):
    ts = h_ref.shape[1]
    si = pl.program_id(1)
    step = pl.program_id(0) * pl.num_programs(1) + si

    @pl.when(step == 0)
    def _():
        un_ref[...] = _modulated(h_ref[0], g_ref[...], shn_ref[0], scn_ref[0])

    u_ref[...] = un_ref[...]

    @pl.when(si == 0)
    def _():
        carry_ref[...] = jnp.zeros_like(carry_ref)

    groups = sorted(range(len(POOL_WINDOWS)), key=lambda g: -POOL_WINDOWS[g])
    span = {g: slice(g * POOL_GROUP_W, (g + 1) * POOL_GROUP_W) for g in groups}
    val = {g: jnp.dot(u_ref[...], win_ref[:, span[g]], preferred_element_type=F32)
           for g in groups}
    un_ref[...] = _modulated(hn_ref[0], g_ref[...], shn_ref[0], scn_ref[0])
    t = si * ts + lax.broadcasted_iota(jnp.int32, (ts, LANES), 0)
    gz = {}
    for g in groups:
        w = POOL_WINDOWS[g]
        z = jnp.dot(u_ref[...], win_ref[:, D_INNER + span[g].start:D_INNER + span[g].stop],
                    preferred_element_type=F32)
        s = jnp.concatenate([carry_ref[:, span[g]], val[g]], axis=0)
        carry_ref[:, span[g]] = val[g][ts - POOL_HALO:, :]
        k = 1
        while k < w:
            s = s + pltpu.roll(s, k, axis=0)
            k *= 2
        s = s[POOL_HALO:, :]
        inv = 1.0 / jnp.minimum(t + 1, w).astype(F32)
        inv = jnp.concatenate([inv] * (POOL_GROUP_W // LANES), axis=1)
        mixed = s * inv - val[g]
        gz[g] = ((mixed * asc_ref[:, span[g]]) * _silu(z)).astype(BF16)
    y = jnp.dot(jnp.concatenate([gz[g] for g in sorted(groups)], axis=1), wout_ref[...],
                preferred_element_type=F32)
    o_ref[0] = h_ref[0] + gt_ref[0] * y


def _a_layer_call(h, shift, scale, gate, g, w_in, a_scale, w_out, layer):
    b, s, d = h.shape
    ts = A_TILE
    n_s = s // ts
    assert n_s > 1
    nxt = _next_tile(b, n_s)
    row = pl.BlockSpec((1, ts, d), lambda bi, si: (bi, si, 0))
    row_next = pl.BlockSpec((1, ts, d), lambda bi, si: (*nxt(bi, si), 0))
    vec = pl.BlockSpec((1, 1, d), lambda bi, si: (bi, 0, 0))
    vec_next = pl.BlockSpec((1, 1, d), lambda bi, si: (nxt(bi, si)[0], 0, 0))
    return pl.pallas_call(
        _a_kernel,
        out_shape=jax.ShapeDtypeStruct(h.shape, F32),
        grid=(b, n_s),
        in_specs=[row, row_next, vec_next, vec_next, vec,
                  _resident((1, d)), _resident(w_in.shape, layer),
                  _resident((1, D_INNER)), _resident(w_out.shape, layer)],
        out_specs=row,
        scratch_shapes=[pltpu.VMEM((POOL_HALO, D_INNER), F32),
                        pltpu.VMEM((ts, d), BF16),
                        pltpu.VMEM((ts, d), BF16)],
        compiler_params=pltpu.CompilerParams(
            dimension_semantics=("arbitrary", "arbitrary"), vmem_limit_bytes=VMEM_LIMIT),
        name="a_layer",
    )(h, h, shift, scale, gate, g, w_in, a_scale, w_out)


def _store_heads(dst_ref, val):
    for hd in range(N_HEADS):
        dst_ref[0, hd] = val[:, hd * HEAD_DIM:(hd + 1) * HEAD_DIM].astype(BF16)


def _bproj_kernel(*refs, with_kv):
    if with_kv:
        (h_ref, sh_ref, sc_ref, g_ref, win_ref, ksh_ref, ksc_ref, kg_ref, wkv_ref,
         q_ref, z_ref, k_ref, v_ref) = refs
    else:
        h_ref, sh_ref, sc_ref, g_ref, win_ref, q_ref, z_ref = refs
    x = h_ref[0]
    xn = x * _rms_scale(x)
    u = xn * (g_ref[...] * (1.0 + sc_ref[0])) + sh_ref[0]
    qz = jnp.dot(u.astype(BF16), win_ref[...], preferred_element_type=F32)
    _store_heads(q_ref, qz[:, :D_INNER] * Q_PRESCALE)
    z_ref[0] = qz[:, D_INNER:]
    if with_kv:
        hk = xn * (kg_ref[...] * (1.0 + ksc_ref[0])) + ksh_ref[0]
        kv = jnp.dot(hk.astype(BF16), wkv_ref[...], preferred_element_type=F32)
        _store_heads(k_ref, kv[:, :D_INNER])
        _store_heads(v_ref, kv[:, D_INNER:])


def _bproj_call(h, shift, scale, g, w_in, layer, kv=None):
    b, s, d = h.shape
    ts = P_TILE
    row = pl.BlockSpec((1, ts, d), lambda bi, si: (bi, si, 0))
    vec = pl.BlockSpec((1, 1, d), lambda bi, si: (bi, 0, 0))
    heads = pl.BlockSpec((1, N_HEADS, ts, HEAD_DIM), lambda bi, si: (bi, 0, si, 0))
    heads_shape = jax.ShapeDtypeStruct((b, N_HEADS, s, HEAD_DIM), BF16)
    in_specs = [row, vec, vec, _resident((1, d)), _resident(w_in.shape, layer)]
    args = [h, shift, scale, g, w_in]
    out_shape = [heads_shape, jax.ShapeDtypeStruct((b, s, D_INNER), F32)]
    out_specs = [heads, pl.BlockSpec((1, ts, D_INNER), lambda bi, si: (bi, si, 0))]
    if kv is not None:
        kshift, kscale, kg, w_kv = kv
        in_specs += [vec, vec, _resident((1, d)), _resident(w_kv.shape)]
        args += [kshift, kscale, kg, w_kv]
        out_shape += [heads_shape, heads_shape]
        out_specs += [heads, heads]
    return pl.pallas_call(
        functools.partial(_bproj_kernel, with_kv=kv is not None),
        out_shape=out_shape,
        grid=(b, s // ts),
        in_specs=in_specs,
        out_specs=out_specs,
        compiler_params=pltpu.CompilerParams(
            dimension_semantics=("arbitrary", "arbitrary"), vmem_limit_bytes=VMEM_LIMIT),
        name="b_proj_kv" if kv is not None else "b_proj",
    )(*args)


def _attn_kernel(*refs, final):
    if final:
        (q_ref, k_ref, v_ref, z_ref, h_ref, gt_ref, bvec_ref, wout_ref, fg_ref,
         o_ref, kring, vring, att_ref, s_scr, p_scr, m_scr, tab_ref) = refs
    else:
        (q_ref, k_ref, v_ref, z_ref, h_ref, gt_ref, bvec_ref, wout_ref,
         o_ref, kring, vring, att_ref, s_scr, p_scr, m_scr, tab_ref) = refs
    i = pl.program_id(1)
    n_keys = KEY_BLOCKS * Q_BLOCK

    @pl.when((pl.program_id(0) == 0) & (i == 0))
    def _():
        def one_head(hd, carry):
            skew = pltpu.roll(jnp.broadcast_to(bvec_ref[hd], (Q_BLOCK, bvec_ref.shape[-1])),
                              0, 1, stride=1, stride_axis=0)
            qc = lax.broadcasted_iota(jnp.int32, (Q_BLOCK, n_keys), 0) // CHUNK
            kc = lax.broadcasted_iota(jnp.int32, (Q_BLOCK, n_keys), 1) // CHUNK
            bias = skew[:, :n_keys] * LOG2E
            bias = jnp.where(kc >= qc, bias, NEG_INF)
            tab_ref[hd] = jnp.where(kc <= qc + LEFT_CHUNKS, bias, NEG_INF)
            return carry

        lax.fori_loop(0, N_HEADS, one_head, 0)

    @pl.when(i == 0)
    def _():
        kring[...] = jnp.zeros_like(kring)
        vring[:, :, :, :HEAD_DIM] = jnp.zeros(vring.shape[:3] + (HEAD_DIM,), BF16)
        vring[:, :, :, HEAD_DIM:] = jnp.ones(vring.shape[:3] + (HEAD_DIM,), BF16)

    slots = [lax.rem(i + 1 + j, KEY_BLOCKS) for j in range(KEY_BLOCKS - 1)]
    cols = [slice(j * Q_BLOCK, (j + 1) * Q_BLOCK) for j in range(KEY_BLOCKS)]
    ones = jnp.ones((Q_BLOCK, HEAD_DIM), BF16)

    def scores(hd, buf, masked):
        qh = q_ref[0, hd]
        keys = jnp.concatenate([kring[s, hd] for s in slots] + [k_ref[0, hd]], axis=0)
        s = lax.dot_general(qh, keys, (((1,), (1,)), ((), ())),
                            preferred_element_type=F32)
        s = s + tab_ref[hd]
        if masked:
            s = jnp.concatenate(
                [jnp.where(i >= KEY_BLOCKS - 1 - j, s[:, cols[j]], NEG_INF)
                 for j in range(KEY_BLOCKS - 1)] + [s[:, cols[KEY_BLOCKS - 1]]], axis=1)
        s_scr[buf] = s
        m_scr[buf] = functools.reduce(
            jnp.maximum, [s[:, c * LANES:(c + 1) * LANES] for c in range(s.shape[1] // LANES)])

    def softmax(buf):
        for r in range(0, Q_BLOCK, SOFTMAX_ROWS):
            rows = slice(r, r + SOFTMAX_ROWS)
            m = m_scr[buf, rows].max(axis=-1, keepdims=True)
            p_scr[buf, rows] = jnp.exp2(s_scr[buf, rows] - m).astype(BF16)

    def values(hd, buf):
        own = jnp.concatenate([v_ref[0, hd], ones], axis=1)
        vals = jnp.concatenate([vring[s, hd] for s in slots] + [own], axis=0)
        oe = jnp.dot(p_scr[buf], vals, preferred_element_type=F32)
        att_ref[hd] = oe[:, :HEAD_DIM] * (1.0 / oe[:, HEAD_DIM:])

    def finish():
        kring[lax.rem(i, KEY_BLOCKS)] = k_ref[0]
        vring[lax.rem(i, KEY_BLOCKS), :, :, :HEAD_DIM] = v_ref[0]
        att =jnp.concatenate([att_ref[hd] for hd in range(N_HEADS)], axis=1)
        gz = att * _silu(z_ref[0])
        y = jnp.dot(gz.astype(BF16), wout_ref[...], preferred_element_type=F32)
        out = h_ref[0] + gt_ref[0] * y
        if final:
            out = out * _rms_scale(out) * fg_ref[...]
        o_ref[0] = out

    def all_heads(masked, unrolled):
        scores(0, 0, masked)
        scores(1, 1, masked)
        softmax(0)

        def step(it, carry):
            for u in range(2):
                hd = 1 + 2 * it + u
                scores(hd + 1, u, masked)
                softmax(1 - u)
                values(hd - 1, u)
            return carry

        if unrolled:
            for it in range((N_HEADS - 2) // 2):
                step(it, 0)
        else:
            lax.fori_loop(0, (N_HEADS - 2) // 2, step, 0)
        softmax(1)
        values(N_HEADS - 2, 0)
        values(N_HEADS - 1, 1)
        finish()

    @pl.when(i >= KEY_BLOCKS - 1)
    def _():
        all_heads(masked=False, unrolled=True)

    @pl.when(i < KEY_BLOCKS - 1)
    def _():
        all_heads(masked=True, unrolled=False)


def _attn_call(q, k, v, z, h, gate, bias_vec, w_out, layer, final_g=None):
    b, s, d = h.shape
    tq = Q_BLOCK
    heads = pl.BlockSpec((1, N_HEADS, tq, HEAD_DIM), lambda bi, i: (bi, 0, i, 0))
    row = pl.BlockSpec((1, tq, d), lambda bi, i: (bi, i, 0))
    in_specs = [heads, heads, heads,
                pl.BlockSpec((1, tq, D_INNER), lambda bi, i: (bi, i, 0)),
                row,
                pl.BlockSpec((1, 1, d), lambda bi, i: (bi, 0, 0)),
                _resident(bias_vec.shape), _resident(w_out.shape, layer)]
    args = [q, k, v, z, h, gate, bias_vec, w_out]
    if final_g is not None:
        in_specs.append(_resident((1, d)))
        args.append(final_g)
    kring = pltpu.VMEM((KEY_BLOCKS, N_HEADS, tq, HEAD_DIM), BF16)
    vring = pltpu.VMEM((KEY_BLOCKS, N_HEADS, tq, 2 * HEAD_DIM), BF16)
    return pl.pallas_call(
        functools.partial(_attn_kernel, final=final_g is not None),
        out_shape=jax.ShapeDtypeStruct(h.shape, F32),
        grid=(b, s // tq),
        in_specs=in_specs,
        out_specs=row,
        scratch_shapes=[kring, vring, pltpu.VMEM((N_HEADS, tq, HEAD_DIM), F32),
                        pltpu.VMEM((2, tq, KEY_BLOCKS * tq), F32),
                        pltpu.VMEM((2, tq, KEY_BLOCKS * tq), BF16),
                        pltpu.VMEM((2, tq, LANES), F32),
                        pltpu.VMEM((N_HEADS, tq, KEY_BLOCKS * tq), F32)],
        compiler_params=pltpu.CompilerParams(
            dimension_semantics=("arbitrary", "arbitrary"), vmem_limit_bytes=VMEM_LIMIT),
        name="band_attn_final" if final_g is not None else "band_attn",
    )(*args)


def _bias_vector(rel_bias):
    n_heads = rel_bias.shape[0]
    n_keys = KEY_BLOCKS * Q_BLOCK
    top = (KEY_BLOCKS - 1) * Q_BLOCK + Q_BLOCK - 1
    n_dist = Q_BLOCK + n_keys - 1
    vec = jnp.concatenate([
        jnp.broadcast_to(rel_bias[:, -1:], (n_heads, top - REL_CLIP)),
        rel_bias[:, ::-1],
        jnp.broadcast_to(rel_bias[:, :1], (n_heads, n_dist - (top - REL_CLIP) - rel_bias.shape[1])),
        jnp.zeros((n_heads, 1), rel_bias.dtype)], axis=1)
    return jnp.roll(vec, -(Q_BLOCK - 1), axis=1).astype(F32)[:, None, :]


@jax.jit
def _forward(x, c, ada_w, ada_b, norm_g, a_w_in, a_w_group, a_scale, a_w_out, kv_norm_g,
             kv_ada_w, kv_ada_b, w_kv, b_w_in, b_rel_bias, b_w_out, final_g):
    b, s, d = x.shape
    depth = ada_w.shape[0]
    n_a = a_w_in.shape[0]
    c_pad = jnp.pad(c, ((0, 8 - b), (0, 0)))
    mods = _ada_call(c_pad, ada_w, ada_b.reshape(depth, 1, 3 * d))[:, :b]
    kvm = _ada_call(c_pad, kv_ada_w[None], kv_ada_b.reshape(1, 1, 2 * d))[0, :b]

    def vec(a):
        return a.reshape(b, 1, d)

    a_w_in = _fold_call(a_w_in, a_w_group)
    a_w_out = a_w_out.astype(BF16)
    b_w_in, b_w_out, w_kv = (w.astype(BF16) for w in (b_w_in, b_w_out, w_kv))

    h = x
    k = v = None
    for layer in range(depth):
        shift, scale, gate = (vec(mods[layer, :, j * d:(j + 1) * d]) for j in range(3))
        g = norm_g[layer].reshape(1, d)
        if layer < n_a:
            h = _a_layer_call(h, shift, scale, gate, g, a_w_in,
                              a_scale[layer].reshape(1, D_INNER), a_w_out, layer)
        else:
            bi = layer - n_a
            if layer == n_a:
                kv = (vec(kvm[:, :d]), vec(kvm[:, d:]), kv_norm_g.reshape(1, d), w_kv)
                q, z, k, v = _bproj_call(h, shift, scale, g, b_w_in, bi, kv)
            else:
                q, z = _bproj_call(h, shift, scale, g, b_w_in, bi)
            last = layer == depth - 1
            h = _attn_call(q, k, v, z, h, gate, _bias_vector(b_rel_bias[bi]), b_w_out, bi,
                           final_g.reshape(1, d) if last else None)
    return h


def kernel(x, c, ada_w, ada_b, norm_g, a_w_in, a_w_group, a_scale, a_w_out, kv_norm_g, kv_ada_w, kv_ada_b, w_kv, b_w_in, b_rel_bias, b_w_out, final_g):
    return _forward(x, c, ada_w, ada_b, norm_g, a_w_in, a_w_group, a_scale, a_w_out,
                    kv_norm_g, kv_ada_w, kv_ada_b, w_kv, b_w_in, b_rel_bias, b_w_out, final_g)
```

```python
import functools

import jax
import jax.numpy as jnp
from jax import lax
from jax.experimental import pallas as pl
from jax.experimental.pallas import tpu as pltpu

F32 = jnp.float32
BF16 = jnp.bfloat16

D_MODEL = 1024
D_INNER = 2048
N_HEADS = 16
HEAD_DIM = 128
CHUNK = 64
LEFT_CHUNKS = 8
REL_CLIP = 128
POOL_WINDOWS = (2, 4, 8, 16)
POOL_GROUP_W = D_INNER // len(POOL_WINDOWS)
POOL_HALO = 16
EPS = 1e-6
LOG2E = 1.4426950408889634
Q_PRESCALE = HEAD_DIM ** -0.5 * LOG2E
NEG_INF = float("-inf")

LANES = 128
Q_BLOCK = 256
KEY_BLOCKS = LEFT_CHUNKS * CHUNK // Q_BLOCK + 1
SOFTMAX_ROWS = 64
A_TILE = 512
P_TILE = 512
VMEM_LIMIT = 54 * 1024 * 1024


def _silu(x):
    return x * (1.0 / (1.0 + jnp.exp(-x)))


def _rms_scale(x):
    return lax.rsqrt(jnp.mean(x * x, axis=-1, keepdims=True) + EPS)


def _modulated(x, g, shift, scale):
    return ((x * _rms_scale(x)) * (g * (1.0 + scale)) + shift).astype(BF16)


def _resident(shape, layer=None):
    if layer is None:
        return pl.BlockSpec(shape, lambda *_: (0,) * len(shape), pipeline_mode=pl.Buffered(1))
    return pl.BlockSpec((None,) + tuple(shape[1:]),
                        lambda *_: (layer,) + (0,) * (len(shape) - 1),
                        pipeline_mode=pl.Buffered(1))


def _next_tile(n_b, n_s):
    def tile(bi, si):
        wrap = si == n_s - 1
        return jnp.minimum(bi + wrap, n_b - 1), jnp.where(wrap, 0, si + 1)
    return tile


def _ada_kernel(c_ref, w_ref, b_ref, o_ref):
    ca = _silu(c_ref[...])
    o_ref[0] = jnp.dot(ca, w_ref[0], preferred_element_type=F32) + b_ref[0]


def _ada_call(c_pad, w, b):
    n_layers, d, n = w.shape
    tn = 1024
    return pl.pallas_call(
        _ada_kernel,
        out_shape=jax.ShapeDtypeStruct((n_layers, c_pad.shape[0], n), F32),
        grid=(n_layers, n // tn),
        in_specs=[
            pl.BlockSpec(c_pad.shape, lambda l, j: (0, 0)),
            pl.BlockSpec((1, d, tn), lambda l, j: (l, 0, j)),
            pl.BlockSpec((1, 1, tn), lambda l, j: (l, 0, j)),
        ],
        out_specs=pl.BlockSpec((1, c_pad.shape[0], tn), lambda l, j: (l, 0, j)),
        compiler_params=pltpu.CompilerParams(
            dimension_semantics=("arbitrary", "arbitrary")),
        name="ada_mod",
    )(c_pad, w, b)


def _fold_kernel(win_ref, wg_ref, o_ref, *, n_groups):
    j = pl.program_id(1)

    def split(x):
        hi = x.astype(BF16)
        return hi, (x - hi.astype(F32)).astype(BF16)

    @pl.when(j < n_groups)
    def _():
        a_hi, a_lo = split(win_ref[0])
        b_hi, b_lo = split(wg_ref[0, 0])
        prod = (jnp.dot(a_hi, b_hi, preferred_element_type=F32)
                + jnp.dot(a_lo, b_hi, preferred_element_type=F32)
                + jnp.dot(a_hi, b_lo, preferred_element_type=F32))
        o_ref[0] = prod.astype(BF16)

    @pl.when(j >= n_groups)
    def _():
        o_ref[0] = win_ref[0].astype(BF16)


def _fold_call(w_in, w_group):
    n_layers, d, n = w_in.shape
    n_groups, gw = w_group.shape[1], w_group.shape[2]
    return pl.pallas_call(
        functools.partial(_fold_kernel, n_groups=n_groups),
        out_shape=jax.ShapeDtypeStruct(w_in.shape, BF16),
        grid=(n_layers, n // gw),
        in_specs=[pl.BlockSpec((1, d, gw), lambda l, j: (l, 0, j)),
                  pl.BlockSpec((1, 1, gw, gw), lambda l, j: (l, jnp.minimum(j, n_groups - 1), 0, 0))],
        out_specs=pl.BlockSpec((1, d, gw), lambda l, j: (l, 0, j)),
        compiler_params=pltpu.CompilerParams(
            dimension_semantics=("arbitrary", "arbitrary")),
        name="fold_group_mix",
    )(w_in, w_group)


def _a_kernel(h_ref, hn_ref, shn_ref, scn_ref, gt_ref, g_ref, win_ref, asc_ref,
              wout_ref, o_ref, carry_ref, u_ref, un_ref):
    ts = h_ref.shape[1]
    si = pl.program_id(1)
    step = pl.program_id(0) * pl.num_programs(1) + si

    @pl.when(step == 0)
    def _():
        un_ref[...] = _modulated(h_ref[0], g_ref[...], shn_ref[0], scn_ref[0])

    u_ref[...] = un_ref[...]

    @pl.when(si == 0)
    def _():
        carry_ref[...] = jnp.zeros_like(carry_ref)

    groups = sorted(range(len(POOL_WINDOWS)), key=lambda g: -POOL_WINDOWS[g])
    span = {g: slice(g * POOL_GROUP_W, (g + 1) * POOL_GROUP_W) for g in groups}
    val = {g: jnp.dot(u_ref[...], win_ref[:, span[g]], preferred_element_type=F32)
           for g in groups}
    un_ref[...] = _modulated(hn_ref[0], g_ref[...], shn_ref[0], scn_ref[0])
    t = si * ts + lax.broadcasted_iota(jnp.int32, (ts, LANES), 0)
    gz = {}
    for g in groups:
        w = POOL_WINDOWS[g]
        z = jnp.dot(u_ref[...], win_ref[:, D_INNER + span[g].start:D_INNER + span[g].stop],
                    preferred_element_type=F32)
        s = jnp.concatenate([carry_ref[:, span[g]], val[g]], axis=0)
        carry_ref[:, span[g]] = val[g][ts - POOL_HALO:, :]
        k = 1
        while k < w:
            s = s + pltpu.roll(s, k, axis=0)
            k *= 2
        s = s[POOL_HALO:, :]
        inv = 1.0 / jnp.minimum(t + 1, w).astype(F32)
        inv = jnp.concatenate([inv] * (POOL_GROUP_W // LANES), axis=1)
        mixed = s * inv - val[g]
        gz[g] = ((mixed * asc_ref[:, span[g]]) * _silu(z)).astype(BF16)
    y = jnp.dot(jnp.concatenate([gz[g] for g in sorted(groups)], axis=1), wout_ref[...],
                preferred_element_type=F32)
    o_ref[0] = h_ref[0] + gt_ref[0] * y


def _a_layer_call(h, shift, scale, gate, g, w_in, a_scale, w_out, layer):
    b, s, d = h.shape
    ts = A_TILE
    n_s = s // ts
    assert n_s > 1
    nxt = _next_tile(b, n_s)
    row = pl.BlockSpec((1, ts, d), lambda bi, si: (bi, si, 0))
    row_next = pl.BlockSpec((1, ts, d), lambda bi, si: (*nxt(bi, si), 0))
    vec = pl.BlockSpec((1, 1, d), lambda bi, si: (bi, 0, 0))
    vec_next = pl.BlockSpec((1, 1, d), lambda bi, si: (nxt(bi, si)[0], 0, 0))
    return pl.pallas_call(
        _a_kernel,
        out_shape=jax.ShapeDtypeStruct(h.shape, F32),
        grid=(b, n_s),
        in_specs=[row, row_next, vec_next, vec_next, vec,
                  _resident((1, d)), _resident(w_in.shape, layer),
                  _resident((1, D_INNER)), _resident(w_out.shape, layer)],
        out_specs=row,
        scratch_shapes=[pltpu.VMEM((POOL_HALO, D_INNER), F32),
                        pltpu.VMEM((ts, d), BF16),
                        pltpu.VMEM((ts, d), BF16)],
        compiler_params=pltpu.CompilerParams(
            dimension_semantics=("arbitrary", "arbitrary"), vmem_limit_bytes=VMEM_LIMIT),
        name="a_layer",
    )(h, h, shift, scale, gate, g, w_in, a_scale, w_out)


def _store_heads(dst_ref, val):
    for hd in range(N_HEADS):
        dst_ref[0, hd] = val[:, hd * HEAD_DIM:(hd + 1) * HEAD_DIM].astype(BF16)


def _bproj_kernel(*refs, with_kv):
    if with_kv:
        (h_ref, sh_ref, sc_ref, g_ref, win_ref, ksh_ref, ksc_ref, kg_ref, wkt_ref, wv_ref,
         q_ref, z_ref, kt_ref, v_ref) = refs
    else:
        h_ref, sh_ref, sc_ref, g_ref, win_ref, q_ref, z_ref = refs
    x = h_ref[0]
    xn = x * _rms_scale(x)
    u = xn * (g_ref[...] * (1.0 + sc_ref[0])) + sh_ref[0]
    qz = jnp.dot(u.astype(BF16), win_ref[...], preferred_element_type=F32)
    _store_heads(q_ref, qz[:, :D_INNER] * Q_PRESCALE)
    z_ref[0] = qz[:, D_INNER:]
    if with_kv:
        hk = (xn * (kg_ref[...] * (1.0 + ksc_ref[0])) + ksh_ref[0]).astype(BF16)
        _store_heads(v_ref, jnp.dot(hk, wv_ref[...], preferred_element_type=F32))
        kt = lax.dot_general(wkt_ref[...], hk, (((1,), (1,)), ((), ())),
                             preferred_element_type=F32)
        for hd in range(N_HEADS):
            kt_ref[0, hd] = kt[hd * HEAD_DIM:(hd + 1) * HEAD_DIM, :].astype(BF16)


def _bproj_call(h, shift, scale, g, w_in, layer, kv=None):
    b, s, d = h.shape
    ts = P_TILE
    row = pl.BlockSpec((1, ts, d), lambda bi, si: (bi, si, 0))
    vec = pl.BlockSpec((1, 1, d), lambda bi, si: (bi, 0, 0))
    heads = pl.BlockSpec((1, N_HEADS, ts, HEAD_DIM), lambda bi, si: (bi, 0, si, 0))
    heads_shape = jax.ShapeDtypeStruct((b, N_HEADS, s, HEAD_DIM), BF16)
    in_specs = [row, vec, vec, _resident((1, d)), _resident(w_in.shape, layer)]
    args = [h, shift, scale, g, w_in]
    out_shape = [heads_shape, jax.ShapeDtypeStruct((b, s, D_INNER), F32)]
    out_specs = [heads, pl.BlockSpec((1, ts, D_INNER), lambda bi, si: (bi, si, 0))]
    if kv is not None:
        kshift, kscale, kg, w_kt, w_v = kv
        in_specs += [vec, vec, _resident((1, d)), _resident(w_kt.shape), _resident(w_v.shape)]
        args += [kshift, kscale, kg, w_kt, w_v]
        out_shape += [jax.ShapeDtypeStruct((b, N_HEADS, HEAD_DIM, s), BF16), heads_shape]
        out_specs += [pl.BlockSpec((1, N_HEADS, HEAD_DIM, ts), lambda bi, si: (bi, 0, 0, si)),
                      heads]
    return pl.pallas_call(
        functools.partial(_bproj_kernel, with_kv=kv is not None),
        out_shape=out_shape,
        grid=(b, s // ts),
        in_specs=in_specs,
        out_specs=out_specs,
        compiler_params=pltpu.CompilerParams(
            dimension_semantics=("arbitrary", "arbitrary"), vmem_limit_bytes=VMEM_LIMIT),
        name="b_proj_kv" if kv is not None else "b_proj",
    )(*args)


def _attn_kernel(*refs, final):
    if final:
        (q_ref, k_ref, v_ref, z_ref, h_ref, gt_ref, bvec_ref, wout_ref, fg_ref,
         o_ref, kring, vring, att_ref, s_scr, p_scr, m_scr, tab_ref) = refs
    else:
        (q_ref, k_ref, v_ref, z_ref, h_ref, gt_ref, bvec_ref, wout_ref,
         o_ref, kring, vring, att_ref, s_scr, p_scr, m_scr, tab_ref) = refs
    i = pl.program_id(1)
    n_keys = KEY_BLOCKS * Q_BLOCK

    @pl.when((pl.program_id(0) == 0) & (i == 0))
    def _():
        def one_head(hd, carry):
            skew = pltpu.roll(jnp.broadcast_to(bvec_ref[hd], (Q_BLOCK, bvec_ref.shape[-1])),
                              0, 1, stride=1, stride_axis=0)
            qc = lax.broadcasted_iota(jnp.int32, (Q_BLOCK, n_keys), 0) // CHUNK
            kc = lax.broadcasted_iota(jnp.int32, (Q_BLOCK, n_keys), 1) // CHUNK
            bias = skew[:, :n_keys] * LOG2E
            bias = jnp.where(kc >= qc, bias, NEG_INF)
            tab_ref[hd] = jnp.where(kc <= qc + LEFT_CHUNKS, bias, NEG_INF)
            return carry

        lax.fori_loop(0, N_HEADS, one_head, 0)

    @pl.when(i == 0)
    def _():
        kring[...] = jnp.zeros_like(kring)
        vring[:, :, :, :HEAD_DIM] = jnp.zeros(vring.shape[:3] + (HEAD_DIM,), BF16)
        vring[:, :, :, HEAD_DIM:] = jnp.ones(vring.shape[:3] + (HEAD_DIM,), BF16)

    slots = [lax.rem(i + 1 + j, KEY_BLOCKS) for j in range(KEY_BLOCKS - 1)]
    cols = [slice(j * Q_BLOCK, (j + 1) * Q_BLOCK) for j in range(KEY_BLOCKS)]
    ones = jnp.ones((Q_BLOCK, HEAD_DIM), BF16)

    def scores(hd, buf, masked):
        qh = q_ref[0, hd]
        keys_t = jnp.concatenate([kring[s, hd] for s in slots] + [k_ref[0, hd]], axis=1)
        s = jnp.dot(qh, keys_t, preferred_element_type=F32)
        s = s + tab_ref[hd]
        if masked:
            s = jnp.concatenate(
                [jnp.where(i >= KEY_BLOCKS - 1 - j, s[:, cols[j]], NEG_INF)
                 for j in range(KEY_BLOCKS - 1)] + [s[:, cols[KEY_BLOCKS - 1]]], axis=1)
        s_scr[buf] = s
        m_scr[buf] = functools.reduce(
            jnp.maximum, [s[:, c * LANES:(c + 1) * LANES] for c in range(s.shape[1] // LANES)])

    def softmax(buf):
        for r in range(0, Q_BLOCK, SOFTMAX_ROWS):
            rows = slice(r, r + SOFTMAX_ROWS)
            m = m_scr[buf, rows].max(axis=-1, keepdims=True)
            p_scr[buf, rows] = jnp.exp2(s_scr[buf, rows] - m).astype(BF16)

    def values(hd, buf):
        own = jnp.concatenate([v_ref[0, hd], ones], axis=1)
        vals = jnp.concatenate([vring[s, hd] for s in slots] + [own], axis=0)
        oe = jnp.dot(p_scr[buf], vals, preferred_element_type=F32)
        att_ref[hd] = oe[:, :HEAD_DIM] * (1.0 / oe[:, HEAD_DIM:])

    def finish():
        kring[lax.rem(i, KEY_BLOCKS)] = k_ref[0]
        vring[lax.rem(i, KEY_BLOCKS), :, :, :HEAD_DIM] = v_ref[0]
        att =jnp.concatenate([att_ref[hd] for hd in range(N_HEADS)], axis=1)
        gz = att * _silu(z_ref[0])
        y = jnp.dot(gz.astype(BF16), wout_ref[...], preferred_element_type=F32)
        out = h_ref[0] + gt_ref[0] * y
        if final:
            out = out * _rms_scale(out) * fg_ref[...]
        o_ref[0] = out

    def all_heads(masked, unrolled):
        scores(0, 0, masked)
        scores(1, 1, masked)
        softmax(0)

        def step(it, carry):
            for u in range(2):
                hd = 1 + 2 * it + u
                scores(hd + 1, u, masked)
                softmax(1 - u)
                values(hd - 1, u)
            return carry

        if unrolled:
            for it in range((N_HEADS - 2) // 2):
                step(it, 0)
        else:
            lax.fori_loop(0, (N_HEADS - 2) // 2, step, 0)
        softmax(1)
        values(N_HEADS - 2, 0)
        values(N_HEADS - 1, 1)
        finish()

    @pl.when(i >= KEY_BLOCKS - 1)
    def _():
        all_heads(masked=False, unrolled=True)

    @pl.when(i < KEY_BLOCKS - 1)
    def _():
        all_heads(masked=True, unrolled=False)


def _attn_call(q, k, v, z, h, gate, bias_vec, w_out, layer, final_g=None):
    b, s, d = h.shape
    tq = Q_BLOCK
    heads = pl.BlockSpec((1, N_HEADS, tq, HEAD_DIM), lambda bi, i: (bi, 0, i, 0))
    row = pl.BlockSpec((1, tq, d), lambda bi, i: (bi, i, 0))
    heads_t = pl.BlockSpec((1, N_HEADS, HEAD_DIM, tq), lambda bi, i: (bi, 0, 0, i))
    in_specs = [heads, heads_t, heads,
                pl.BlockSpec((1, tq, D_INNER), lambda bi, i: (bi, i, 0)),
                row,
                pl.BlockSpec((1, 1, d), lambda bi, i: (bi, 0, 0)),
                _resident(bias_vec.shape), _resident(w_out.shape, layer)]
    args = [q, k, v, z, h, gate, bias_vec, w_out]
    if final_g is not None:
        in_specs.append(_resident((1, d)))
        args.append(final_g)
    kring = pltpu.VMEM((KEY_BLOCKS, N_HEADS, HEAD_DIM, tq), BF16)
    vring = pltpu.VMEM((KEY_BLOCKS, N_HEADS, tq, 2 * HEAD_DIM), BF16)
    return pl.pallas_call(
        functools.partial(_attn_kernel, final=final_g is not None),
        out_shape=jax.ShapeDtypeStruct(h.shape, F32),
        grid=(b, s // tq),
        in_specs=in_specs,
        out_specs=row,
        scratch_shapes=[kring, vring, pltpu.VMEM((N_HEADS, tq, HEAD_DIM), F32),
                        pltpu.VMEM((2, tq, KEY_BLOCKS * tq), F32),
                        pltpu.VMEM((2, tq, KEY_BLOCKS * tq), BF16),
                        pltpu.VMEM((2, tq, LANES), F32),
                        pltpu.VMEM((N_HEADS, tq, KEY_BLOCKS * tq), F32)],
        compiler_params=pltpu.CompilerParams(
            dimension_semantics=("arbitrary", "arbitrary"), vmem_limit_bytes=VMEM_LIMIT),
        name="band_attn_final" if final_g is not None else "band_attn",
    )(*args)


def _bias_vector(rel_bias):
    n_heads = rel_bias.shape[0]
    n_keys = KEY_BLOCKS * Q_BLOCK
    top = (KEY_BLOCKS - 1) * Q_BLOCK + Q_BLOCK - 1
    n_dist = Q_BLOCK + n_keys - 1
    vec = jnp.concatenate([
        jnp.broadcast_to(rel_bias[:, -1:], (n_heads, top - REL_CLIP)),
        rel_bias[:, ::-1],
        jnp.broadcast_to(rel_bias[:, :1], (n_heads, n_dist - (top - REL_CLIP) - rel_bias.shape[1])),
        jnp.zeros((n_heads, 1), rel_bias.dtype)], axis=1)
    return jnp.roll(vec, -(Q_BLOCK - 1), axis=1).astype(F32)[:, None, :]


@jax.jit
def _forward(x, c, ada_w, ada_b, norm_g, a_w_in, a_w_group, a_scale, a_w_out, kv_norm_g,
             kv_ada_w, kv_ada_b, w_kv, b_w_in, b_rel_bias, b_w_out, final_g):
    b, s, d = x.shape
    depth = ada_w.shape[0]
    n_a = a_w_in.shape[0]
    c_pad = jnp.pad(c, ((0, 8 - b), (0, 0)))
    mods = _ada_call(c_pad, ada_w, ada_b.reshape(depth, 1, 3 * d))[:, :b]
    kvm = _ada_call(c_pad, kv_ada_w[None], kv_ada_b.reshape(1, 1, 2 * d))[0, :b]

    def vec(a):
        return a.reshape(b, 1, d)

    a_w_in = _fold_call(a_w_in, a_w_group)
    a_w_out = a_w_out.astype(BF16)
    b_w_in, b_w_out = b_w_in.astype(BF16), b_w_out.astype(BF16)
    w_kt = w_kv[:, :D_INNER].T.astype(BF16)
    w_v = w_kv[:, D_INNER:].astype(BF16)

    h = x
    k = v = None
    for layer in range(depth):
        shift, scale, gate = (vec(mods[layer, :, j * d:(j + 1) * d]) for j in range(3))
        g = norm_g[layer].reshape(1, d)
        if layer < n_a:
            h = _a_layer_call(h, shift, scale, gate, g, a_w_in,
                              a_scale[layer].reshape(1, D_INNER), a_w_out, layer)
        else:
            bi = layer - n_a
            if layer == n_a:
                kv = (vec(kvm[:, :d]), vec(kvm[:, d:]), kv_norm_g.reshape(1, d), w_kt, w_v)
                q, z, k, v = _bproj_call(h, shift, scale, g, b_w_in, bi, kv)
            else:
                q, z = _bproj_call(h, shift, scale, g, b_w_in, bi)
            last = layer == depth - 1
            h = _attn_call(q, k, v, z, h, gate, _bias_vector(b_rel_bias[bi]), b_w_out, bi,
                           final_g.reshape(1, d) if last else None)
    return h


def kernel(x, c, ada_w, ada_b, norm_g, a_w_in, a_w_group, a_scale, a_w_out, kv_norm_g, kv_ada_w, kv_ada_b, w_kv, b_w_in, b_rel_bias, b_w_out, final_g):
    return _forward(x, c, ada_w, ada_b, norm_g, a_w_in, a_w_group, a_scale, a_w_out,
                    kv_norm_g, kv_ada_w, kv_ada_b, w_kv, b_w_in, b_rel_bias, b_w_out, final_g)
```

```python
import functools

import jax
import jax.numpy as jnp
from jax import lax
from jax.experimental import pallas as pl
from jax.experimental.pallas import tpu as pltpu

F32 = jnp.float32
BF16 = jnp.bfloat16

D_MODEL = 1024
D_INNER = 2048
N_HEADS = 16
HEAD_DIM = 128
CHUNK = 64
LEFT_CHUNKS = 8
REL_CLIP = 128
POOL_WINDOWS = (2, 4, 8, 16)
POOL_GROUP_W = D_INNER // len(POOL_WINDOWS)
POOL_HALO = 16
EPS = 1e-6
LOG2E = 1.4426950408889634
Q_PRESCALE = HEAD_DIM ** -0.5 * LOG2E
NEG_INF = float("-inf")

LANES = 128
Q_BLOCK = 256
KEY_BLOCKS = LEFT_CHUNKS * CHUNK // Q_BLOCK + 1
SOFTMAX_ROWS = 64
A_TILE = 512
P_TILE = 512
VMEM_LIMIT = 54 * 1024 * 1024


def _silu(x):
    return x * (1.0 / (1.0 + jnp.exp(-x)))


def _rms_scale(x):
    return lax.rsqrt(jnp.mean(x * x, axis=-1, keepdims=True) + EPS)


def _modulated(x, g, shift, scale):
    return ((x * _rms_scale(x)) * (g * (1.0 + scale)) + shift).astype(BF16)


def _resident(shape, layer=None):
    if layer is None:
        return pl.BlockSpec(shape, lambda *_: (0,) * len(shape), pipeline_mode=pl.Buffered(1))
    return pl.BlockSpec((None,) + tuple(shape[1:]),
                        lambda *_: (layer,) + (0,) * (len(shape) - 1),
                        pipeline_mode=pl.Buffered(1))


def _next_tile(n_b, n_s):
    def tile(bi, si):
        wrap = si == n_s - 1
        return jnp.minimum(bi + wrap, n_b - 1), jnp.where(wrap, 0, si + 1)
    return tile


def _ada_kernel(c_ref, w_ref, b_ref, o_ref):
    ca = _silu(c_ref[...])
    o_ref[0] = jnp.dot(ca, w_ref[0], preferred_element_type=F32) + b_ref[0]


def _ada_call(c_pad, w, b):
    n_layers, d, n = w.shape
    tn = 1024
    return pl.pallas_call(
        _ada_kernel,
        out_shape=jax.ShapeDtypeStruct((n_layers, c_pad.shape[0], n), F32),
        grid=(n_layers, n // tn),
        in_specs=[
            pl.BlockSpec(c_pad.shape, lambda l, j: (0, 0)),
            pl.BlockSpec((1, d, tn), lambda l, j: (l, 0, j)),
            pl.BlockSpec((1, 1, tn), lambda l, j: (l, 0, j)),
        ],
        out_specs=pl.BlockSpec((1, c_pad.shape[0], tn), lambda l, j: (l, 0, j)),
        compiler_params=pltpu.CompilerParams(
            dimension_semantics=("arbitrary", "arbitrary")),
        name="ada_mod",
    )(c_pad, w, b)


def _fold_kernel(win_ref, wg_ref, o_ref, *, n_groups):
    j = pl.program_id(1)

    def split(x):
        hi = x.astype(BF16)
        return hi, (x - hi.astype(F32)).astype(BF16)

    @pl.when(j < n_groups)
    def _():
        a_hi, a_lo = split(win_ref[0])
        b_hi, b_lo = split(wg_ref[0, 0])
        prod = (jnp.dot(a_hi, b_hi, preferred_element_type=F32)
                + jnp.dot(a_lo, b_hi, preferred_element_type=F32)
                + jnp.dot(a_hi, b_lo, preferred_element_type=F32))
        o_ref[0] = prod.astype(BF16)

    @pl.when(j >= n_groups)
    def _():
        o_ref[0] = win_ref[0].astype(BF16)


def _fold_call(w_in, w_group):
    n_layers, d, n = w_in.shape
    n_groups, gw = w_group.shape[1], w_group.shape[2]
    return pl.pallas_call(
        functools.partial(_fold_kernel, n_groups=n_groups),
        out_shape=jax.ShapeDtypeStruct(w_in.shape, BF16),
        grid=(n_layers, n // gw),
        in_specs=[pl.BlockSpec((1, d, gw), lambda l, j: (l, 0, j)),
                  pl.BlockSpec((1, 1, gw, gw), lambda l, j: (l, jnp.minimum(j, n_groups - 1), 0, 0))],
        out_specs=pl.BlockSpec((1, d, gw), lambda l, j: (l, 0, j)),
        compiler_params=pltpu.CompilerParams(
            dimension_semantics=("arbitrary", "arbitrary")),
        name="fold_group_mix",
    )(w_in, w_group)


def _a_kernel(h_ref, hn_ref, shn_ref, scn_ref, gt_ref, g_ref, win_ref, asc_ref,
              wout_ref, o_ref, carry_ref, u_ref, un_ref):
    ts = h_ref.shape[1]
    si = pl.program_id(1)
    step = pl.program_id(0) * pl.num_programs(1) + si

    @pl.when(step == 0)
    def _():
        un_ref[...] = _modulated(h_ref[0], g_ref[...], shn_ref[0], scn_ref[0])

    u_ref[...] = un_ref[...]

    @pl.when(si == 0)
    def _():
        carry_ref[...] = jnp.zeros_like(carry_ref)

    groups = sorted(range(len(POOL_WINDOWS)), key=lambda g: -POOL_WINDOWS[g])
    span = {g: slice(g * POOL_GROUP_W, (g + 1) * POOL_GROUP_W) for g in groups}
    val = {g: jnp.dot(u_ref[...], win_ref[:, span[g]], preferred_element_type=F32)
           for g in groups}
    un_ref[...] = _modulated(hn_ref[0], g_ref[...], shn_ref[0], scn_ref[0])
    t = si * ts + lax.broadcasted_iota(jnp.int32, (ts, LANES), 0)
    gz = {}
    for g in groups:
        w = POOL_WINDOWS[g]
        z = jnp.dot(u_ref[...], win_ref[:, D_INNER + span[g].start:D_INNER + span[g].stop],
                    preferred_element_type=F32)
        s = jnp.concatenate([carry_ref[:, span[g]], val[g]], axis=0)
        carry_ref[:, span[g]] = val[g][ts - POOL_HALO:, :]
        k = 1
        while k < w:
            s = s + pltpu.roll(s, k, axis=0)
            k *= 2
        s = s[POOL_HALO:, :]
        inv = 1.0 / jnp.minimum(t + 1, w).astype(F32)
        inv = jnp.concatenate([inv] * (POOL_GROUP_W // LANES), axis=1)
        mixed = s * inv - val[g]
        gz[g] = ((mixed * asc_ref[:, span[g]]) * _silu(z)).astype(BF16)
    y = jnp.dot(jnp.concatenate([gz[g] for g in sorted(groups)], axis=1), wout_ref[...],
                preferred_element_type=F32)
    o_ref[0] = h_ref[0] + gt_ref[0] * y


def _a_layer_call(h, shift, scale, gate, g, w_in, a_scale, w_out, layer):
    b, s, d = h.shape
    ts = A_TILE
    n_s = s // ts
    assert n_s > 1
    nxt = _next_tile(b, n_s)
    row = pl.BlockSpec((1, ts, d), lambda bi, si: (bi, si, 0))
    row_next = pl.BlockSpec((1, ts, d), lambda bi, si: (*nxt(bi, si), 0))
    vec = pl.BlockSpec((1, 1, d), lambda bi, si: (bi, 0, 0))
    vec_next = pl.BlockSpec((1, 1, d), lambda bi, si: (nxt(bi, si)[0], 0, 0))
    return pl.pallas_call(
        _a_kernel,
        out_shape=jax.ShapeDtypeStruct(h.shape, F32),
        grid=(b, n_s),
        in_specs=[row, row_next, vec_next, vec_next, vec,
                  _resident((1, d)), _resident(w_in.shape, layer),
                  _resident((1, D_INNER)), _resident(w_out.shape, layer)],
        out_specs=row,
        scratch_shapes=[pltpu.VMEM((POOL_HALO, D_INNER), F32),
                        pltpu.VMEM((ts, d), BF16),
                        pltpu.VMEM((ts, d), BF16)],
        compiler_params=pltpu.CompilerParams(
            dimension_semantics=("arbitrary", "arbitrary"), vmem_limit_bytes=VMEM_LIMIT),
        name="a_layer",
    )(h, h, shift, scale, gate, g, w_in, a_scale, w_out)


def _store_heads(dst_ref, val):
    for hd in range(N_HEADS):
        dst_ref[0, hd] = val[:, hd * HEAD_DIM:(hd + 1) * HEAD_DIM].astype(BF16)


def _bproj_kernel(*refs, with_kv):
    if with_kv:
        (h_ref, sh_ref, sc_ref, g_ref, win_ref, ksh_ref, ksc_ref, kg_ref, wkt_ref, wv_ref,
         q_ref, z_ref, kt_ref, v_ref) = refs
    else:
        h_ref, sh_ref, sc_ref, g_ref, win_ref, q_ref, z_ref = refs
    x = h_ref[0]
    xn = x * _rms_scale(x)
    u = xn * (g_ref[...] * (1.0 + sc_ref[0])) + sh_ref[0]
    qz = jnp.dot(u.astype(BF16), win_ref[...], preferred_element_type=F32)
    _store_heads(q_ref, qz[:, :D_INNER] * Q_PRESCALE)
    z_ref[0] = qz[:, D_INNER:]
    if with_kv:
        hk = (xn * (kg_ref[...] * (1.0 + ksc_ref[0])) + ksh_ref[0]).astype(BF16)
        _store_heads(v_ref, jnp.dot(hk, wv_ref[...], preferred_element_type=F32))
        kt = lax.dot_general(wkt_ref[...], hk, (((1,), (1,)), ((), ())),
                             preferred_element_type=F32)
        for hd in range(N_HEADS):
            kt_ref[0, hd] = kt[hd * HEAD_DIM:(hd + 1) * HEAD_DIM, :].astype(BF16)


def _bproj_call(h, shift, scale, g, w_in, layer, kv=None):
    b, s, d = h.shape
    ts = P_TILE
    row = pl.BlockSpec((1, ts, d), lambda bi, si: (bi, si, 0))
    vec = pl.BlockSpec((1, 1, d), lambda bi, si: (bi, 0, 0))
    heads = pl.BlockSpec((1, N_HEADS, ts, HEAD_DIM), lambda bi, si: (bi, 0, si, 0))
    heads_shape = jax.ShapeDtypeStruct((b, N_HEADS, s, HEAD_DIM), BF16)
    in_specs = [row, vec, vec, _resident((1, d)), _resident(w_in.shape, layer)]
    args = [h, shift, scale, g, w_in]
    out_shape = [heads_shape, jax.ShapeDtypeStruct((b, s, D_INNER), F32)]
    out_specs = [heads, pl.BlockSpec((1, ts, D_INNER), lambda bi, si: (bi, si, 0))]
    if kv is not None:
        kshift, kscale, kg, w_kt, w_v = kv
        in_specs += [vec, vec, _resident((1, d)), _resident(w_kt.shape), _resident(w_v.shape)]
        args += [kshift, kscale, kg, w_kt, w_v]
        out_shape += [jax.ShapeDtypeStruct((b, N_HEADS, HEAD_DIM, s), BF16), heads_shape]
        out_specs += [pl.BlockSpec((1, N_HEADS, HEAD_DIM, ts), lambda bi, si: (bi, 0, 0, si)),
                      heads]
    return pl.pallas_call(
        functools.partial(_bproj_kernel, with_kv=kv is not None),
        out_shape=out_shape,
        grid=(b, s // ts),
        in_specs=in_specs,
        out_specs=out_specs,
        compiler_params=pltpu.CompilerParams(
            dimension_semantics=("arbitrary", "arbitrary"), vmem_limit_bytes=VMEM_LIMIT),
        name="b_proj_kv" if kv is not None else "b_proj",
    )(*args)


def _attn_kernel(*refs, final):
    if final:
        (q_ref, k_ref, v_ref, z_ref, h_ref, gt_ref, bvec_ref, wout_ref, fg_ref,
         o_ref, kring, vring, att_ref, s_scr, p_scr, m_scr, tab_ref) = refs
    else:
        (q_ref, k_ref, v_ref, z_ref, h_ref, gt_ref, bvec_ref, wout_ref,
         o_ref, kring, vring, att_ref, s_scr, p_scr, m_scr, tab_ref) = refs
    i = pl.program_id(1)
    n_keys = KEY_BLOCKS * Q_BLOCK

    @pl.when((pl.program_id(0) == 0) & (i == 0))
    def _():
        def one_head(hd, carry):
            skew = pltpu.roll(jnp.broadcast_to(bvec_ref[hd], (Q_BLOCK, bvec_ref.shape[-1])),
                              0, 1, stride=1, stride_axis=0)
            qc = lax.broadcasted_iota(jnp.int32, (Q_BLOCK, n_keys), 0) // CHUNK
            kc = lax.broadcasted_iota(jnp.int32, (Q_BLOCK, n_keys), 1) // CHUNK
            bias = skew[:, :n_keys] * LOG2E
            bias = jnp.where(kc >= qc, bias, NEG_INF)
            tab_ref[hd] = jnp.where(kc <= qc + LEFT_CHUNKS, bias, NEG_INF)
            return carry

        lax.fori_loop(0, N_HEADS, one_head, 0)
        vring[:, :, :, HEAD_DIM:] = jnp.ones(vring.shape[:3] + (HEAD_DIM,), BF16)

    slots = [lax.rem(i + 1 + j, KEY_BLOCKS) for j in range(KEY_BLOCKS - 1)]
    ones = jnp.ones((Q_BLOCK, HEAD_DIM), BF16)

    def scores(hd, buf, live):
        qh = q_ref[0, hd]
        older = slots[KEY_BLOCKS - live:]
        keys_t = jnp.concatenate([kring[s, hd] for s in older] + [k_ref[0, hd]], axis=1)
        s = jnp.dot(qh, keys_t, preferred_element_type=F32)
        s = s + tab_ref[hd, :, (KEY_BLOCKS - live) * Q_BLOCK:]
        s_scr[buf, :, :live * Q_BLOCK] = s
        m_scr[buf] = functools.reduce(
            jnp.maximum, [s[:, c * LANES:(c + 1) * LANES] for c in range(s.shape[1] // LANES)])

    def softmax(buf, live):
        width = live * Q_BLOCK
        for r in range(0, Q_BLOCK, SOFTMAX_ROWS):
            rows = slice(r, r + SOFTMAX_ROWS)
            m = m_scr[buf, rows].max(axis=-1, keepdims=True)
            p_scr[buf, rows, :width] = jnp.exp2(s_scr[buf, rows, :width] - m).astype(BF16)

    def values(hd, buf, live):
        older = slots[KEY_BLOCKS - live:]
        own = jnp.concatenate([v_ref[0, hd], ones], axis=1)
        vals = jnp.concatenate([vring[s, hd] for s in older] + [own], axis=0)
        oe = jnp.dot(p_scr[buf, :, :live * Q_BLOCK], vals,
                     preferred_element_type=F32)
        att_ref[hd] = oe[:, :HEAD_DIM] * (1.0 / oe[:, HEAD_DIM:])

    def finish():
        kring[lax.rem(i, KEY_BLOCKS)] = k_ref[0]
        vring[lax.rem(i, KEY_BLOCKS), :, :, :HEAD_DIM] = v_ref[0]
        att = jnp.concatenate([att_ref[hd] for hd in range(N_HEADS)], axis=1)
        gz = att * _silu(z_ref[0])
        y = jnp.dot(gz.astype(BF16), wout_ref[...], preferred_element_type=F32)
        out = h_ref[0] + gt_ref[0] * y
        if final:
            out = out * _rms_scale(out) * fg_ref[...]
        o_ref[0] = out

    def all_heads(live):
        scores(0, 0, live)
        scores(1, 1, live)
        softmax(0, live)
        for hd in range(1, N_HEADS - 1):
            scores(hd + 1, (hd + 1) % 2, live)
            softmax(hd % 2, live)
            values(hd - 1, (hd - 1) % 2, live)
        softmax((N_HEADS - 1) % 2, live)
        values(N_HEADS - 2, (N_HEADS - 2) % 2, live)
        values(N_HEADS - 1, (N_HEADS - 1) % 2, live)
        finish()

    for live in range(1, KEY_BLOCKS):
        @pl.when(i == live - 1)
        def _(live=live):
            all_heads(live)

    @pl.when(i >= KEY_BLOCKS - 1)
    def _():
        all_heads(KEY_BLOCKS)


def _attn_call(q, k, v, z, h, gate, bias_vec, w_out, layer, final_g=None):
    b, s, d = h.shape
    tq = Q_BLOCK
    heads = pl.BlockSpec((1, N_HEADS, tq, HEAD_DIM), lambda bi, i: (bi, 0, i, 0))
    row = pl.BlockSpec((1, tq, d), lambda bi, i: (bi, i, 0))
    heads_t = pl.BlockSpec((1, N_HEADS, HEAD_DIM, tq), lambda bi, i: (bi, 0, 0, i))
    in_specs = [heads, heads_t, heads,
                pl.BlockSpec((1, tq, D_INNER), lambda bi, i: (bi, i, 0)),
                row,
                pl.BlockSpec((1, 1, d), lambda bi, i: (bi, 0, 0)),
                _resident(bias_vec.shape), _resident(w_out.shape, layer)]
    args = [q, k, v, z, h, gate, bias_vec, w_out]
    if final_g is not None:
        in_specs.append(_resident((1, d)))
        args.append(final_g)
    kring = pltpu.VMEM((KEY_BLOCKS, N_HEADS, HEAD_DIM, tq), BF16)
    vring = pltpu.VMEM((KEY_BLOCKS, N_HEADS, tq, 2 * HEAD_DIM), BF16)
    return pl.pallas_call(
        functools.partial(_attn_kernel, final=final_g is not None),
        out_shape=jax.ShapeDtypeStruct(h.shape, F32),
        grid=(b, s // tq),
        in_specs=in_specs,
        out_specs=row,
        scratch_shapes=[kring, vring, pltpu.VMEM((N_HEADS, tq, HEAD_DIM), F32),
                        pltpu.VMEM((2, tq, KEY_BLOCKS * tq), F32),
                        pltpu.VMEM((2, tq, KEY_BLOCKS * tq), BF16),
                        pltpu.VMEM((2, tq, LANES), F32),
                        pltpu.VMEM((N_HEADS, tq, KEY_BLOCKS * tq), F32)],
        compiler_params=pltpu.CompilerParams(
            dimension_semantics=("arbitrary", "arbitrary"), vmem_limit_bytes=VMEM_LIMIT),
        name="band_attn_final" if final_g is not None else "band_attn",
    )(*args)


def _bias_vector(rel_bias):
    n_heads = rel_bias.shape[0]
    n_keys = KEY_BLOCKS * Q_BLOCK
    top = (KEY_BLOCKS - 1) * Q_BLOCK + Q_BLOCK - 1
    n_dist = Q_BLOCK + n_keys - 1
    vec = jnp.concatenate([
        jnp.broadcast_to(rel_bias[:, -1:], (n_heads, top - REL_CLIP)),
        rel_bias[:, ::-1],
        jnp.broadcast_to(rel_bias[:, :1], (n_heads, n_dist - (top - REL_CLIP) - rel_bias.shape[1])),
        jnp.zeros((n_heads, 1), rel_bias.dtype)], axis=1)
    return jnp.roll(vec, -(Q_BLOCK - 1), axis=1).astype(F32)[:, None, :]


@jax.jit
def _forward(x, c, ada_w, ada_b, norm_g, a_w_in, a_w_group, a_scale, a_w_out, kv_norm_g,
             kv_ada_w, kv_ada_b, w_kv, b_w_in, b_rel_bias, b_w_out, final_g):
    b, s, d = x.shape
    depth = ada_w.shape[0]
    n_a = a_w_in.shape[0]
    c_pad = jnp.pad(c, ((0, 8 - b), (0, 0)))
    mods = _ada_call(c_pad, ada_w, ada_b.reshape(depth, 1, 3 * d))[:, :b]
    kvm = _ada_call(c_pad, kv_ada_w[None], kv_ada_b.reshape(1, 1, 2 * d))[0, :b]

    def vec(a):
        return a.reshape(b, 1, d)

    a_w_in = _fold_call(a_w_in, a_w_group)
    a_w_out = a_w_out.astype(BF16)
    b_w_in, b_w_out = b_w_in.astype(BF16), b_w_out.astype(BF16)
    w_kt = w_kv[:, :D_INNER].T.astype(BF16)
    w_v = w_kv[:, D_INNER:].astype(BF16)

    h = x
    k = v = None
    for layer in range(depth):
        shift, scale, gate = (vec(mods[layer, :, j * d:(j + 1) * d]) for j in range(3))
        g = norm_g[layer].reshape(1, d)
        if layer < n_a:
            h = _a_layer_call(h, shift, scale, gate, g, a_w_in,
                              a_scale[layer].reshape(1, D_INNER), a_w_out, layer)
        else:
            bi = layer - n_a
            if layer == n_a:
                kv = (vec(kvm[:, :d]), vec(kvm[:, d:]), kv_norm_g.reshape(1, d), w_kt, w_v)
                q, z, k, v = _bproj_call(h, shift, scale, g, b_w_in, bi, kv)
            else:
                q, z = _bproj_call(h, shift, scale, g, b_w_in, bi)
            last = layer == depth - 1
            h = _attn_call(q, k, v, z, h, gate, _bias_vector(b_rel_bias[bi]), b_w_out, bi,
                           final_g.reshape(1, d) if last else None)
    return h


def kernel(x, c, ada_w, ada_b, norm_g, a_w_in, a_w_group, a_scale, a_w_out, kv_norm_g, kv_ada_w, kv_ada_b, w_kv, b_w_in, b_rel_bias, b_w_out, final_g):
    return _forward(x, c, ada_w, ada_b, norm_g, a_w_in, a_w_group, a_scale, a_w_out,
                    kv_norm_g, kv_ada_w, kv_ada_b, w_kv, b_w_in, b_rel_bias, b_w_out, final_g)
```

```python
import functools

import jax
import jax.numpy as jnp
from jax import lax
from jax.experimental import pallas as pl
from jax.experimental.pallas import tpu as pltpu

F32 = jnp.float32
BF16 = jnp.bfloat16

D_MODEL = 1024
D_INNER = 2048
N_HEADS = 16
HEAD_DIM = 128
CHUNK = 64
LEFT_CHUNKS = 8
REL_CLIP = 128
POOL_WINDOWS = (2, 4, 8, 16)
POOL_GROUP_W = D_INNER // len(POOL_WINDOWS)
POOL_HALO = 16
EPS = 1e-6
LOG2E = 1.4426950408889634
Q_PRESCALE = HEAD_DIM ** -0.5 * LOG2E
NEG_INF = float("-inf")

LANES = 128
Q_BLOCK = 256
KEY_BLOCKS = LEFT_CHUNKS * CHUNK // Q_BLOCK + 1
SOFTMAX_ROWS = 64
A_TILE = 512
A_STREAMS = 2
P_TILE = 512
VMEM_LIMIT = 54 * 1024 * 1024


def _silu(x):
    return x * (1.0 / (1.0 + jnp.exp(-x)))


def _rms_scale(x):
    return lax.rsqrt(jnp.mean(x * x, axis=-1, keepdims=True) + EPS)


def _modulated(x, g, shift, scale):
    return ((x * _rms_scale(x)) * (g * (1.0 + scale)) + shift).astype(BF16)


def _resident(shape, layer=None):
    if layer is None:
        return pl.BlockSpec(shape, lambda *_: (0,) * len(shape), pipeline_mode=pl.Buffered(1))
    return pl.BlockSpec((None,) + tuple(shape[1:]),
                        lambda *_: (layer,) + (0,) * (len(shape) - 1),
                        pipeline_mode=pl.Buffered(1))


def _ada_kernel(c_ref, w_ref, b_ref, o_ref):
    ca = _silu(c_ref[...])
    o_ref[0] = jnp.dot(ca, w_ref[0], preferred_element_type=F32) + b_ref[0]


def _ada_call(c_pad, w, b):
    n_layers, d, n = w.shape
    tn = 1024
    return pl.pallas_call(
        _ada_kernel,
        out_shape=jax.ShapeDtypeStruct((n_layers, c_pad.shape[0], n), F32),
        grid=(n_layers, n // tn),
        in_specs=[
            pl.BlockSpec(c_pad.shape, lambda l, j: (0, 0)),
            pl.BlockSpec((1, d, tn), lambda l, j: (l, 0, j)),
            pl.BlockSpec((1, 1, tn), lambda l, j: (l, 0, j)),
        ],
        out_specs=pl.BlockSpec((1, c_pad.shape[0], tn), lambda l, j: (l, 0, j)),
        compiler_params=pltpu.CompilerParams(
            dimension_semantics=("arbitrary", "arbitrary")),
        name="ada_mod",
    )(c_pad, w, b)


def _fold_kernel(win_ref, wg_ref, o_ref, *, n_groups):
    j = pl.program_id(1)

    def split(x):
        hi = x.astype(BF16)
        return hi, (x - hi.astype(F32)).astype(BF16)

    @pl.when(j < n_groups)
    def _():
        a_hi, a_lo = split(win_ref[0])
        b_hi, b_lo = split(wg_ref[0, 0])
        prod = (jnp.dot(a_hi, b_hi, preferred_element_type=F32)
                + jnp.dot(a_lo, b_hi, preferred_element_type=F32)
                + jnp.dot(a_hi, b_lo, preferred_element_type=F32))
        o_ref[0] = prod.astype(BF16)

    @pl.when(j >= n_groups)
    def _():
        o_ref[0] = win_ref[0].astype(BF16)


def _fold_call(w_in, w_group):
    n_layers, d, n = w_in.shape
    n_groups, gw = w_group.shape[1], w_group.shape[2]
    return pl.pallas_call(
        functools.partial(_fold_kernel, n_groups=n_groups),
        out_shape=jax.ShapeDtypeStruct(w_in.shape, BF16),
        grid=(n_layers, n // gw),
        in_specs=[pl.BlockSpec((1, d, gw), lambda l, j: (l, 0, j)),
                  pl.BlockSpec((1, 1, gw, gw), lambda l, j: (l, jnp.minimum(j, n_groups - 1), 0, 0))],
        out_specs=pl.BlockSpec((1, d, gw), lambda l, j: (l, 0, j)),
        compiler_params=pltpu.CompilerParams(
            dimension_semantics=("arbitrary", "arbitrary")),
        name="fold_group_mix",
    )(w_in, w_group)


def _a_kernel(h_ref, sh_ref, sc_ref, gt_ref, g_ref, win_ref, asc_ref, wout_ref,
              o_ref, carry_ref, u_ref):
    n_streams, ts = h_ref.shape[0], h_ref.shape[1]
    streams = range(n_streams)
    si = pl.program_id(1)

    @pl.when(si == 0)
    def _():
        carry_ref[...] = jnp.zeros_like(carry_ref)

    rows = [slice(n * ts, (n + 1) * ts) for n in streams]
    for n in streams:
        u_ref[rows[n]] = _modulated(h_ref[n], g_ref[...], sh_ref[n], sc_ref[n])

    groups = sorted(range(len(POOL_WINDOWS)), key=lambda g: -POOL_WINDOWS[g])
    span = {g: slice(g * POOL_GROUP_W, (g + 1) * POOL_GROUP_W) for g in groups}
    val = {g: jnp.dot(u_ref[...], win_ref[:, span[g]], preferred_element_type=F32)
           for g in groups}
    t = si * ts + lax.broadcasted_iota(jnp.int32, (ts, LANES), 0)
    gz = {}
    for g in groups:
        w = POOL_WINDOWS[g]
        inv = 1.0 / jnp.minimum(t + 1, w).astype(F32)
        inv = jnp.concatenate([inv] * (POOL_GROUP_W // LANES), axis=1)
        z_all = jnp.dot(u_ref[...], win_ref[:, D_INNER + span[g].start:D_INNER + span[g].stop],
                        preferred_element_type=F32)
        for n in streams:
            z = z_all[rows[n]]
            v = val[g][rows[n]]
            s = jnp.concatenate([carry_ref[n, :, span[g]], v], axis=0)
            carry_ref[n, :, span[g]] = v[ts - POOL_HALO:, :]
            k = 1
            while k < w:
                s = s + pltpu.roll(s, k, axis=0)
                k *= 2
            mixed = s[POOL_HALO:, :] * inv - v
            gz[n, g] = ((mixed * asc_ref[:, span[g]]) * _silu(z)).astype(BF16)
    gated = jnp.concatenate(
        [jnp.concatenate([gz[n, g] for g in sorted(groups)], axis=1) for n in streams], axis=0)
    y = jnp.dot(gated, wout_ref[...], preferred_element_type=F32)
    for n in streams:
        o_ref[n] = h_ref[n] + gt_ref[n] * y[rows[n]]


def _a_layer_call(h, shift, scale, gate, g, w_in, a_scale, w_out, layer):
    b, s, d = h.shape
    ts, ns = A_TILE, A_STREAMS
    row = pl.BlockSpec((ns, ts, d), lambda bi, si: (bi, si, 0))
    vec = pl.BlockSpec((ns, 1, d), lambda bi, si: (bi, 0, 0))
    return pl.pallas_call(
        _a_kernel,
        out_shape=jax.ShapeDtypeStruct(h.shape, F32),
        grid=(b // ns, s // ts),
        in_specs=[row, vec, vec, vec,
                  _resident((1, d)), _resident(w_in.shape, layer),
                  _resident((1, D_INNER)), _resident(w_out.shape, layer)],
        out_specs=row,
        scratch_shapes=[pltpu.VMEM((ns, POOL_HALO, D_INNER), F32),
                        pltpu.VMEM((ns * ts, d), BF16)],
        compiler_params=pltpu.CompilerParams(
            dimension_semantics=("arbitrary", "arbitrary"), vmem_limit_bytes=VMEM_LIMIT),
        name="a_layer",
    )(h, shift, scale, gate, g, w_in, a_scale, w_out)


def _store_heads(dst_ref, val):
    for hd in range(N_HEADS):
        dst_ref[0, hd] = val[:, hd * HEAD_DIM:(hd + 1) * HEAD_DIM].astype(BF16)


def _bproj_kernel(*refs, with_kv):
    if with_kv:
        (h_ref, sh_ref, sc_ref, g_ref, win_ref, ksh_ref, ksc_ref, kg_ref, wkt_ref, wv_ref,
         q_ref, z_ref, kt_ref, v_ref) = refs
    else:
        h_ref, sh_ref, sc_ref, g_ref, win_ref, q_ref, z_ref = refs
    x = h_ref[0]
    xn = x * _rms_scale(x)
    u = xn * (g_ref[...] * (1.0 + sc_ref[0])) + sh_ref[0]
    qz = jnp.dot(u.astype(BF16), win_ref[...], preferred_element_type=F32)
    _store_heads(q_ref, qz[:, :D_INNER] * Q_PRESCALE)
    z_ref[0] = qz[:, D_INNER:]
    if with_kv:
        hk = (xn * (kg_ref[...] * (1.0 + ksc_ref[0])) + ksh_ref[0]).astype(BF16)
        _store_heads(v_ref, jnp.dot(hk, wv_ref[...], preferred_element_type=F32))
        kt = lax.dot_general(wkt_ref[...], hk, (((1,), (1,)), ((), ())),
                             preferred_element_type=F32)
        for hd in range(N_HEADS):
            kt_ref[0, hd] = kt[hd * HEAD_DIM:(hd + 1) * HEAD_DIM, :].astype(BF16)


def _bproj_call(h, shift, scale, g, w_in, layer, kv=None):
    b, s, d = h.shape
    ts = P_TILE
    row = pl.BlockSpec((1, ts, d), lambda bi, si: (bi, si, 0))
    vec = pl.BlockSpec((1, 1, d), lambda bi, si: (bi, 0, 0))
    heads = pl.BlockSpec((1, N_HEADS, ts, HEAD_DIM), lambda bi, si: (bi, 0, si, 0))
    heads_shape = jax.ShapeDtypeStruct((b, N_HEADS, s, HEAD_DIM), BF16)
    in_specs = [row, vec, vec, _resident((1, d)), _resident(w_in.shape, layer)]
    args = [h, shift, scale, g, w_in]
    out_shape = [heads_shape, jax.ShapeDtypeStruct((b, s, D_INNER), F32)]
    out_specs = [heads, pl.BlockSpec((1, ts, D_INNER), lambda bi, si: (bi, si, 0))]
    if kv is not None:
        kshift, kscale, kg, w_kt, w_v = kv
        in_specs += [vec, vec, _resident((1, d)), _resident(w_kt.shape), _resident(w_v.shape)]
        args += [kshift, kscale, kg, w_kt, w_v]
        out_shape += [jax.ShapeDtypeStruct((b, N_HEADS, HEAD_DIM, s), BF16), heads_shape]
        out_specs += [pl.BlockSpec((1, N_HEADS, HEAD_DIM, ts), lambda bi, si: (bi, 0, 0, si)),
                      heads]
    return pl.pallas_call(
        functools.partial(_bproj_kernel, with_kv=kv is not None),
        out_shape=out_shape,
        grid=(b, s // ts),
        in_specs=in_specs,
        out_specs=out_specs,
        compiler_params=pltpu.CompilerParams(
            dimension_semantics=("arbitrary", "arbitrary"), vmem_limit_bytes=VMEM_LIMIT),
        name="b_proj_kv" if kv is not None else "b_proj",
    )(*args)


def _attn_kernel(*refs, final):
    if final:
        (q_ref, k_ref, v_ref, z_ref, h_ref, gt_ref, bvec_ref, wout_ref, fg_ref,
         o_ref, kring, vring, att_ref, s_scr, p_scr, m_scr, tab_ref) = refs
    else:
        (q_ref, k_ref, v_ref, z_ref, h_ref, gt_ref, bvec_ref, wout_ref,
         o_ref, kring, vring, att_ref, s_scr, p_scr, m_scr, tab_ref) = refs
    i = pl.program_id(1)
    n_keys = KEY_BLOCKS * Q_BLOCK

    @pl.when((pl.program_id(0) == 0) & (i == 0))
    def _():
        def one_head(hd, carry):
            skew = pltpu.roll(jnp.broadcast_to(bvec_ref[hd], (Q_BLOCK, bvec_ref.shape[-1])),
                              0, 1, stride=1, stride_axis=0)
            qc = lax.broadcasted_iota(jnp.int32, (Q_BLOCK, n_keys), 0) // CHUNK
            kc = lax.broadcasted_iota(jnp.int32, (Q_BLOCK, n_keys), 1) // CHUNK
            bias = skew[:, :n_keys] * LOG2E
            bias = jnp.where(kc >= qc, bias, NEG_INF)
            tab_ref[hd] = jnp.where(kc <= qc + LEFT_CHUNKS, bias, NEG_INF)
            return carry

        lax.fori_loop(0, N_HEADS, one_head, 0)
        vring[:, :, :, HEAD_DIM:] = jnp.ones(vring.shape[:3] + (HEAD_DIM,), BF16)

    slots = [lax.rem(i + 1 + j, KEY_BLOCKS) for j in range(KEY_BLOCKS - 1)]
    ones = jnp.ones((Q_BLOCK, HEAD_DIM), BF16)

    def scores(hd, buf, live):
        qh = q_ref[0, hd]
        older = slots[KEY_BLOCKS - live:]
        keys_t = jnp.concatenate([kring[s, hd] for s in older] + [k_ref[0, hd]], axis=1)
        s = jnp.dot(qh, keys_t, preferred_element_type=F32)
        s = s + tab_ref[hd, :, (KEY_BLOCKS - live) * Q_BLOCK:]
        s_scr[buf, :, :live * Q_BLOCK] = s
        m_scr[buf] = functools.reduce(
            jnp.maximum, [s[:, c * LANES:(c + 1) * LANES] for c in range(s.shape[1] // LANES)])

    def softmax(buf, live):
        width = live * Q_BLOCK
        for r in range(0, Q_BLOCK, SOFTMAX_ROWS):
            rows = slice(r, r + SOFTMAX_ROWS)
            m = m_scr[buf, rows].max(axis=-1, keepdims=True)
            p_scr[buf, rows, :width] = jnp.exp2(s_scr[buf, rows, :width] - m).astype(BF16)

    def values(hd, buf, live):
        older = slots[KEY_BLOCKS - live:]
        own = jnp.concatenate([v_ref[0, hd], ones], axis=1)
        vals = jnp.concatenate([vring[s, hd] for s in older] + [own], axis=0)
        oe = jnp.dot(p_scr[buf, :, :live * Q_BLOCK], vals,
                     preferred_element_type=F32)
        att_ref[hd] = oe[:, :HEAD_DIM] * (1.0 / oe[:, HEAD_DIM:])

    def finish():
        kring[lax.rem(i, KEY_BLOCKS)] = k_ref[0]
        vring[lax.rem(i, KEY_BLOCKS), :, :, :HEAD_DIM] = v_ref[0]
        att = jnp.concatenate([att_ref[hd] for hd in range(N_HEADS)], axis=1)
        gz = att * _silu(z_ref[0])
        y = jnp.dot(gz.astype(BF16), wout_ref[...], preferred_element_type=F32)
        out = h_ref[0] + gt_ref[0] * y
        if final:
            out = out * _rms_scale(out) * fg_ref[...]
        o_ref[0] = out

    def all_heads(live):
        scores(0, 0, live)
        scores(1, 1, live)
        softmax(0, live)
        for hd in range(1, N_HEADS - 1):
            scores(hd + 1, (hd + 1) % 2, live)
            softmax(hd % 2, live)
            values(hd - 1, (hd - 1) % 2, live)
        softmax((N_HEADS - 1) % 2, live)
        values(N_HEADS - 2, (N_HEADS - 2) % 2, live)
        values(N_HEADS - 1, (N_HEADS - 1) % 2, live)
        finish()

    for live in range(1, KEY_BLOCKS):
        @pl.when(i == live - 1)
        def _(live=live):
            all_heads(live)

    @pl.when(i >= KEY_BLOCKS - 1)
    def _():
        all_heads(KEY_BLOCKS)


def _attn_call(q, k, v, z, h, gate, bias_vec, w_out, layer, final_g=None):
    b, s, d = h.shape
    tq = Q_BLOCK
    heads = pl.BlockSpec((1, N_HEADS, tq, HEAD_DIM), lambda bi, i: (bi, 0, i, 0))
    row = pl.BlockSpec((1, tq, d), lambda bi, i: (bi, i, 0))
    heads_t = pl.BlockSpec((1, N_HEADS, HEAD_DIM, tq), lambda bi, i: (bi, 0, 0, i))
    in_specs = [heads, heads_t, heads,
                pl.BlockSpec((1, tq, D_INNER), lambda bi, i: (bi, i, 0)),
                row,
                pl.BlockSpec((1, 1, d), lambda bi, i: (bi, 0, 0)),
                _resident(bias_vec.shape), _resident(w_out.shape, layer)]
    args = [q, k, v, z, h, gate, bias_vec, w_out]
    if final_g is not None:
        in_specs.append(_resident((1, d)))
        args.append(final_g)
    kring = pltpu.VMEM((KEY_BLOCKS, N_HEADS, HEAD_DIM, tq), BF16)
    vring = pltpu.VMEM((KEY_BLOCKS, N_HEADS, tq, 2 * HEAD_DIM), BF16)
    return pl.pallas_call(
        functools.partial(_attn_kernel, final=final_g is not None),
        out_shape=jax.ShapeDtypeStruct(h.shape, F32),
        grid=(b, s // tq),
        in_specs=in_specs,
        out_specs=row,
        scratch_shapes=[kring, vring, pltpu.VMEM((N_HEADS, tq, HEAD_DIM), F32),
                        pltpu.VMEM((2, tq, KEY_BLOCKS * tq), F32),
                        pltpu.VMEM((2, tq, KEY_BLOCKS * tq), BF16),
                        pltpu.VMEM((2, tq, LANES), F32),
                        pltpu.VMEM((N_HEADS, tq, KEY_BLOCKS * tq), F32)],
        compiler_params=pltpu.CompilerParams(
            dimension_semantics=("arbitrary", "arbitrary"), vmem_limit_bytes=VMEM_LIMIT),
        name="band_attn_final" if final_g is not None else "band_attn",
    )(*args)


def _bias_vector(rel_bias):
    n_heads = rel_bias.shape[0]
    n_keys = KEY_BLOCKS * Q_BLOCK
    top = (KEY_BLOCKS - 1) * Q_BLOCK + Q_BLOCK - 1
    n_dist = Q_BLOCK + n_keys - 1
    vec = jnp.concatenate([
        jnp.broadcast_to(rel_bias[:, -1:], (n_heads, top - REL_CLIP)),
        rel_bias[:, ::-1],
        jnp.broadcast_to(rel_bias[:, :1], (n_heads, n_dist - (top - REL_CLIP) - rel_bias.shape[1])),
        jnp.zeros((n_heads, 1), rel_bias.dtype)], axis=1)
    return jnp.roll(vec, -(Q_BLOCK - 1), axis=1).astype(F32)[:, None, :]


@jax.jit
def _forward(x, c, ada_w, ada_b, norm_g, a_w_in, a_w_group, a_scale, a_w_out, kv_norm_g,
             kv_ada_w, kv_ada_b, w_kv, b_w_in, b_rel_bias, b_w_out, final_g):
    b, s, d = x.shape
    depth = ada_w.shape[0]
    n_a = a_w_in.shape[0]
    c_pad = jnp.pad(c, ((0, 8 - b), (0, 0)))
    mods = _ada_call(c_pad, ada_w, ada_b.reshape(depth, 1, 3 * d))[:, :b]
    kvm = _ada_call(c_pad, kv_ada_w[None], kv_ada_b.reshape(1, 1, 2 * d))[0, :b]

    def vec(a):
        return a.reshape(b, 1, d)

    a_w_in = _fold_call(a_w_in, a_w_group)
    a_w_out = a_w_out.astype(BF16)
    b_w_in, b_w_out = b_w_in.astype(BF16), b_w_out.astype(BF16)
    w_kt = w_kv[:, :D_INNER].T.astype(BF16)
    w_v = w_kv[:, D_INNER:].astype(BF16)

    h = x
    k = v = None
    for layer in range(depth):
        shift, scale, gate = (vec(mods[layer, :, j * d:(j + 1) * d]) for j in range(3))
        g = norm_g[layer].reshape(1, d)
        if layer < n_a:
            h = _a_layer_call(h, shift, scale, gate, g, a_w_in,
                              a_scale[layer].reshape(1, D_INNER), a_w_out, layer)
        else:
            bi = layer - n_a
            if layer == n_a:
                kv = (vec(kvm[:, :d]), vec(kvm[:, d:]), kv_norm_g.reshape(1, d), w_kt, w_v)
                q, z, k, v = _bproj_call(h, shift, scale, g, b_w_in, bi, kv)
            else:
                q, z = _bproj_call(h, shift, scale, g, b_w_in, bi)
            last = layer == depth - 1
            h = _attn_call(q, k, v, z, h, gate, _bias_vector(b_rel_bias[bi]), b_w_out, bi,
                           final_g.reshape(1, d) if last else None)
    return h


def kernel(x, c, ada_w, ada_b, norm_g, a_w_in, a_w_group, a_scale, a_w_out, kv_norm_g, kv_ada_w, kv_ada_b, w_kv, b_w_in, b_rel_bias, b_w_out, final_g):
    return _forward(x, c, ada_w, ada_b, norm_g, a_w_in, a_w_group, a_scale, a_w_out,
                    kv_norm_g, kv_ada_w, kv_ada_b, w_kv, b_w_in, b_rel_bias, b_w_out, final_g)
```

```python
import functools

import jax
import jax.numpy as jnp
from jax import lax
from jax.experimental import pallas as pl
from jax.experimental.pallas import tpu as pltpu

F32 = jnp.float32
BF16 = jnp.bfloat16

D_MODEL = 1024
D_INNER = 2048
N_HEADS = 16
HEAD_DIM = 128
CHUNK = 64
LEFT_CHUNKS = 8
REL_CLIP = 128
POOL_WINDOWS = (2, 4, 8, 16)
POOL_GROUP_W = D_INNER // len(POOL_WINDOWS)
POOL_HALO = 16
EPS = 1e-6
LOG2E = 1.4426950408889634
Q_PRESCALE = HEAD_DIM ** -0.5 * LOG2E
NEG_INF = float("-inf")

LANES = 128
Q_BLOCK = 256
KEY_BLOCKS = LEFT_CHUNKS * CHUNK // Q_BLOCK + 1
SOFTMAX_ROWS = 64
A_TILE = 512
A_STREAMS = 2
P_TILE = 512
P_STREAMS = 2
VMEM_LIMIT = 54 * 1024 * 1024


def _silu(x):
    return x * (1.0 / (1.0 + jnp.exp(-x)))


def _rms_scale(x):
    return lax.rsqrt(jnp.mean(x * x, axis=-1, keepdims=True) + EPS)


def _modulated(x, g, shift, scale):
    return ((x * _rms_scale(x)) * (g * (1.0 + scale)) + shift).astype(BF16)


def _resident(shape, layer=None):
    if layer is None:
        return pl.BlockSpec(shape, lambda *_: (0,) * len(shape), pipeline_mode=pl.Buffered(1))
    return pl.BlockSpec((None,) + tuple(shape[1:]),
                        lambda *_: (layer,) + (0,) * (len(shape) - 1),
                        pipeline_mode=pl.Buffered(1))


def _ada_kernel(c_ref, w_ref, b_ref, o_ref):
    ca = _silu(c_ref[...])
    o_ref[0] = jnp.dot(ca, w_ref[0], preferred_element_type=F32) + b_ref[0]


def _ada_call(c_pad, w, b):
    n_layers, d, n = w.shape
    tn = 1024
    return pl.pallas_call(
        _ada_kernel,
        out_shape=jax.ShapeDtypeStruct((n_layers, c_pad.shape[0], n), F32),
        grid=(n_layers, n // tn),
        in_specs=[
            pl.BlockSpec(c_pad.shape, lambda l, j: (0, 0)),
            pl.BlockSpec((1, d, tn), lambda l, j: (l, 0, j)),
            pl.BlockSpec((1, 1, tn), lambda l, j: (l, 0, j)),
        ],
        out_specs=pl.BlockSpec((1, c_pad.shape[0], tn), lambda l, j: (l, 0, j)),
        compiler_params=pltpu.CompilerParams(
            dimension_semantics=("arbitrary", "arbitrary")),
        name="ada_mod",
    )(c_pad, w, b)


def _fold_kernel(win_ref, wg_ref, o_ref, *, n_groups):
    j = pl.program_id(1)

    def split(x):
        hi = x.astype(BF16)
        return hi, (x - hi.astype(F32)).astype(BF16)

    @pl.when(j < n_groups)
    def _():
        a_hi, a_lo = split(win_ref[0])
        b_hi, b_lo = split(wg_ref[0, 0])
        prod = (jnp.dot(a_hi, b_hi, preferred_element_type=F32)
                + jnp.dot(a_lo, b_hi, preferred_element_type=F32)
                + jnp.dot(a_hi, b_lo, preferred_element_type=F32))
        o_ref[0] = prod.astype(BF16)

    @pl.when(j >= n_groups)
    def _():
        o_ref[0] = win_ref[0].astype(BF16)


def _fold_call(w_in, w_group):
    n_layers, d, n = w_in.shape
    n_groups, gw = w_group.shape[1], w_group.shape[2]
    return pl.pallas_call(
        functools.partial(_fold_kernel, n_groups=n_groups),
        out_shape=jax.ShapeDtypeStruct(w_in.shape, BF16),
        grid=(n_layers, n // gw),
        in_specs=[pl.BlockSpec((1, d, gw), lambda l, j: (l, 0, j)),
                  pl.BlockSpec((1, 1, gw, gw), lambda l, j: (l, jnp.minimum(j, n_groups - 1), 0, 0))],
        out_specs=pl.BlockSpec((1, d, gw), lambda l, j: (l, 0, j)),
        compiler_params=pltpu.CompilerParams(
            dimension_semantics=("arbitrary", "arbitrary")),
        name="fold_group_mix",
    )(w_in, w_group)


def _a_kernel(h_ref, sh_ref, sc_ref, gt_ref, g_ref, win_ref, asc_ref, wout_ref,
              o_ref, carry_ref, u_ref):
    n_streams, ts = h_ref.shape[0], h_ref.shape[1]
    streams = range(n_streams)
    si = pl.program_id(1)

    @pl.when(si == 0)
    def _():
        carry_ref[...] = jnp.zeros_like(carry_ref)

    rows = [slice(n * ts, (n + 1) * ts) for n in streams]
    for n in streams:
        u_ref[rows[n]] = _modulated(h_ref[n], g_ref[...], sh_ref[n], sc_ref[n])

    groups = sorted(range(len(POOL_WINDOWS)), key=lambda g: -POOL_WINDOWS[g])
    span = {g: slice(g * POOL_GROUP_W, (g + 1) * POOL_GROUP_W) for g in groups}
    val = {g: jnp.dot(u_ref[...], win_ref[:, span[g]], preferred_element_type=F32)
           for g in groups}
    t = si * ts + lax.broadcasted_iota(jnp.int32, (ts, LANES), 0)
    gz = {}
    for g in groups:
        w = POOL_WINDOWS[g]
        inv = 1.0 / jnp.minimum(t + 1, w).astype(F32)
        inv = jnp.concatenate([inv] * (POOL_GROUP_W // LANES), axis=1)
        z_all = jnp.dot(u_ref[...], win_ref[:, D_INNER + span[g].start:D_INNER + span[g].stop],
                        preferred_element_type=F32)
        for n in streams:
            z = z_all[rows[n]]
            v = val[g][rows[n]]
            s = jnp.concatenate([carry_ref[n, :, span[g]], v], axis=0)
            carry_ref[n, :, span[g]] = v[ts - POOL_HALO:, :]
            k = 1
            while k < w:
                s = s + pltpu.roll(s, k, axis=0)
                k *= 2
            mixed = s[POOL_HALO:, :] * inv - v
            gz[n, g] = ((mixed * asc_ref[:, span[g]]) * _silu(z)).astype(BF16)
    gated = jnp.concatenate(
        [jnp.concatenate([gz[n, g] for g in sorted(groups)], axis=1) for n in streams], axis=0)
    y = jnp.dot(gated, wout_ref[...], preferred_element_type=F32)
    for n in streams:
        o_ref[n] = h_ref[n] + gt_ref[n] * y[rows[n]]


def _a_layer_call(h, shift, scale, gate, g, w_in, a_scale, w_out, layer):
    b, s, d = h.shape
    ts, ns = A_TILE, A_STREAMS
    row = pl.BlockSpec((ns, ts, d), lambda bi, si: (bi, si, 0))
    vec = pl.BlockSpec((ns, 1, d), lambda bi, si: (bi, 0, 0))
    return pl.pallas_call(
        _a_kernel,
        out_shape=jax.ShapeDtypeStruct(h.shape, F32),
        grid=(b // ns, s // ts),
        in_specs=[row, vec, vec, vec,
                  _resident((1, d)), _resident(w_in.shape, layer),
                  _resident((1, D_INNER)), _resident(w_out.shape, layer)],
        out_specs=row,
        scratch_shapes=[pltpu.VMEM((ns, POOL_HALO, D_INNER), F32),
                        pltpu.VMEM((ns * ts, d), BF16)],
        compiler_params=pltpu.CompilerParams(
            dimension_semantics=("arbitrary", "arbitrary"), vmem_limit_bytes=VMEM_LIMIT),
        name="a_layer",
    )(h, shift, scale, gate, g, w_in, a_scale, w_out)


def _store_heads(dst_ref, n, val):
    for hd in range(N_HEADS):
        dst_ref[n, hd] = val[:, hd * HEAD_DIM:(hd + 1) * HEAD_DIM].astype(BF16)


def _bproj_kernel(*refs, with_kv):
    if with_kv:
        (h_ref, sh_ref, sc_ref, g_ref, win_ref, ksh_ref, ksc_ref, kg_ref, wkt_ref, wv_ref,
         q_ref, z_ref, kt_ref, v_ref) = refs
    else:
        h_ref, sh_ref, sc_ref, g_ref, win_ref, q_ref, z_ref = refs
    n_streams, ts = h_ref.shape[0], h_ref.shape[1]
    streams = range(n_streams)
    rows = [slice(n * ts, (n + 1) * ts) for n in streams]
    xn = [h_ref[n] * _rms_scale(h_ref[n]) for n in streams]
    u = jnp.concatenate([(xn[n] * (g_ref[...] * (1.0 + sc_ref[n])) + sh_ref[n]).astype(BF16)
                         for n in streams], axis=0)
    q = jnp.dot(u, win_ref[:, :D_INNER], preferred_element_type=F32)
    for n in streams:
        _store_heads(q_ref, n, q[rows[n]] * Q_PRESCALE)
    z = jnp.dot(u, win_ref[:, D_INNER:], preferred_element_type=F32)
    for n in streams:
        z_ref[n] = z[rows[n]]
    if with_kv:
        hk = jnp.concatenate([(xn[n] * (kg_ref[...] * (1.0 + ksc_ref[n])) + ksh_ref[n]).astype(BF16)
                              for n in streams], axis=0)
        v = jnp.dot(hk, wv_ref[...], preferred_element_type=F32)
        kt = lax.dot_general(wkt_ref[...], hk, (((1,), (1,)), ((), ())),
                             preferred_element_type=F32)
        for n in streams:
            _store_heads(v_ref, n, v[rows[n]])
            for hd in range(N_HEADS):
                kt_ref[n, hd] = kt[hd * HEAD_DIM:(hd + 1) * HEAD_DIM, rows[n]].astype(BF16)


def _bproj_call(h, shift, scale, g, w_in, layer, kv=None):
    b, s, d = h.shape
    ts = P_TILE
    ns = 1 if kv is not None else P_STREAMS
    row = pl.BlockSpec((ns, ts, d), lambda bi, si: (bi, si, 0))
    vec = pl.BlockSpec((ns, 1, d), lambda bi, si: (bi, 0, 0))
    heads = pl.BlockSpec((ns, N_HEADS, ts, HEAD_DIM), lambda bi, si: (bi, 0, si, 0))
    heads_shape = jax.ShapeDtypeStruct((b, N_HEADS, s, HEAD_DIM), BF16)
    in_specs = [row, vec, vec, _resident((1, d)), _resident(w_in.shape, layer)]
    args = [h, shift, scale, g, w_in]
    out_shape = [heads_shape, jax.ShapeDtypeStruct((b, s, D_INNER), F32)]
    out_specs = [heads, pl.BlockSpec((ns, ts, D_INNER), lambda bi, si: (bi, si, 0))]
    if kv is not None:
        kshift, kscale, kg, w_kt, w_v = kv
        in_specs += [vec, vec, _resident((1, d)), _resident(w_kt.shape), _resident(w_v.shape)]
        args += [kshift, kscale, kg, w_kt, w_v]
        out_shape += [jax.ShapeDtypeStruct((b, N_HEADS, HEAD_DIM, s), BF16), heads_shape]
        out_specs += [pl.BlockSpec((ns, N_HEADS, HEAD_DIM, ts), lambda bi, si: (bi, 0, 0, si)),
                      heads]
    return pl.pallas_call(
        functools.partial(_bproj_kernel, with_kv=kv is not None),
        out_shape=out_shape,
        grid=(b // ns, s // ts),
        in_specs=in_specs,
        out_specs=out_specs,
        compiler_params=pltpu.CompilerParams(
            dimension_semantics=("arbitrary", "arbitrary"), vmem_limit_bytes=VMEM_LIMIT),
        name="b_proj_kv" if kv is not None else "b_proj",
    )(*args)


def _attn_kernel(*refs, final):
    if final:
        (q_ref, k_ref, v_ref, z_ref, h_ref, gt_ref, bvec_ref, wout_ref, fg_ref,
         o_ref, kring, vring, att_ref, s_scr, p_scr, m_scr, tab_ref) = refs
    else:
        (q_ref, k_ref, v_ref, z_ref, h_ref, gt_ref, bvec_ref, wout_ref,
         o_ref, kring, vring, att_ref, s_scr, p_scr, m_scr, tab_ref) = refs
    i = pl.program_id(1)
    n_keys = KEY_BLOCKS * Q_BLOCK

    @pl.when((pl.program_id(0) == 0) & (i == 0))
    def _():
        def one_head(hd, carry):
            skew = pltpu.roll(jnp.broadcast_to(bvec_ref[hd], (Q_BLOCK, bvec_ref.shape[-1])),
                              0, 1, stride=1, stride_axis=0)
            qc = lax.broadcasted_iota(jnp.int32, (Q_BLOCK, n_keys), 0) // CHUNK
            kc = lax.broadcasted_iota(jnp.int32, (Q_BLOCK, n_keys), 1) // CHUNK
            bias = skew[:, :n_keys] * LOG2E
            bias = jnp.where(kc >= qc, bias, NEG_INF)
            tab_ref[hd] = jnp.where(kc <= qc + LEFT_CHUNKS, bias, NEG_INF)
            return carry

        lax.fori_loop(0, N_HEADS, one_head, 0)
        vring[:, :, :, HEAD_DIM:] = jnp.ones(vring.shape[:3] + (HEAD_DIM,), BF16)

    slots = [lax.rem(i + 1 + j, KEY_BLOCKS) for j in range(KEY_BLOCKS - 1)]
    ones = jnp.ones((Q_BLOCK, HEAD_DIM), BF16)

    def scores(hd, buf, live):
        qh = q_ref[0, hd]
        older = slots[KEY_BLOCKS - live:]
        keys_t = jnp.concatenate([kring[s, hd] for s in older] + [k_ref[0, hd]], axis=1)
        s = jnp.dot(qh, keys_t, preferred_element_type=F32)
        s = s + tab_ref[hd, :, (KEY_BLOCKS - live) * Q_BLOCK:]
        s_scr[buf, :, :live * Q_BLOCK] = s
        m_scr[buf] = functools.reduce(
            jnp.maximum, [s[:, c * LANES:(c + 1) * LANES] for c in range(s.shape[1] // LANES)])

    def softmax(buf, live):
        width = live * Q_BLOCK
        for r in range(0, Q_BLOCK, SOFTMAX_ROWS):
            rows = slice(r, r + SOFTMAX_ROWS)
            m = m_scr[buf, rows].max(axis=-1, keepdims=True)
            p_scr[buf, rows, :width] = jnp.exp2(s_scr[buf, rows, :width] - m).astype(BF16)

    def values(hd, buf, live):
        older = slots[KEY_BLOCKS - live:]
        own = jnp.concatenate([v_ref[0, hd], ones], axis=1)
        vals = jnp.concatenate([vring[s, hd] for s in older] + [own], axis=0)
        oe = jnp.dot(p_scr[buf, :, :live * Q_BLOCK], vals,
                     preferred_element_type=F32)
        att_ref[hd] = oe[:, :HEAD_DIM] * (1.0 / oe[:, HEAD_DIM:])

    def finish():
        kring[lax.rem(i, KEY_BLOCKS)] = k_ref[0]
        vring[lax.rem(i, KEY_BLOCKS), :, :, :HEAD_DIM] = v_ref[0]
        att = jnp.concatenate([att_ref[hd] for hd in range(N_HEADS)], axis=1)
        gz = att * _silu(z_ref[0])
        y = jnp.dot(gz.astype(BF16), wout_ref[...], preferred_element_type=F32)
        out = h_ref[0] + gt_ref[0] * y
        if final:
            out = out * _rms_scale(out) * fg_ref[...]
        o_ref[0] = out

    def all_heads(live):
        scores(0, 0, live)
        scores(1, 1, live)
        softmax(0, live)
        for hd in range(1, N_HEADS - 1):
            scores(hd + 1, (hd + 1) % 2, live)
            softmax(hd % 2, live)
            values(hd - 1, (hd - 1) % 2, live)
        softmax((N_HEADS - 1) % 2, live)
        values(N_HEADS - 2, (N_HEADS - 2) % 2, live)
        values(N_HEADS - 1, (N_HEADS - 1) % 2, live)
        finish()

    for live in range(1, KEY_BLOCKS):
        @pl.when(i == live - 1)
        def _(live=live):
            all_heads(live)

    @pl.when(i >= KEY_BLOCKS - 1)
    def _():
        all_heads(KEY_BLOCKS)


def _attn_call(q, k, v, z, h, gate, bias_vec, w_out, layer, final_g=None):
    b, s, d = h.shape
    tq = Q_BLOCK
    heads = pl.BlockSpec((1, N_HEADS, tq, HEAD_DIM), lambda bi, i: (bi, 0, i, 0))
    row = pl.BlockSpec((1, tq, d), lambda bi, i: (bi, i, 0))
    heads_t = pl.BlockSpec((1, N_HEADS, HEAD_DIM, tq), lambda bi, i: (bi, 0, 0, i))
    in_specs = [heads, heads_t, heads,
                pl.BlockSpec((1, tq, D_INNER), lambda bi, i: (bi, i, 0)),
                row,
                pl.BlockSpec((1, 1, d), lambda bi, i: (bi, 0, 0)),
                _resident(bias_vec.shape), _resident(w_out.shape, layer)]
    args = [q, k, v, z, h, gate, bias_vec, w_out]
    if final_g is not None:
        in_specs.append(_resident((1, d)))
        args.append(final_g)
    kring = pltpu.VMEM((KEY_BLOCKS, N_HEADS, HEAD_DIM, tq), BF16)
    vring = pltpu.VMEM((KEY_BLOCKS, N_HEADS, tq, 2 * HEAD_DIM), BF16)
    return pl.pallas_call(
        functools.partial(_attn_kernel, final=final_g is not None),
        out_shape=jax.ShapeDtypeStruct(h.shape, F32),
        grid=(b, s // tq),
        in_specs=in_specs,
        out_specs=row,
        scratch_shapes=[kring, vring, pltpu.VMEM((N_HEADS, tq, HEAD_DIM), F32),
                        pltpu.VMEM((2, tq, KEY_BLOCKS * tq), F32),
                        pltpu.VMEM((2, tq, KEY_BLOCKS * tq), BF16),
                        pltpu.VMEM((2, tq, LANES), F32),
                        pltpu.VMEM((N_HEADS, tq, KEY_BLOCKS * tq), F32)],
        compiler_params=pltpu.CompilerParams(
            dimension_semantics=("arbitrary", "arbitrary"), vmem_limit_bytes=VMEM_LIMIT),
        name="band_attn_final" if final_g is not None else "band_attn",
    )(*args)


def _bias_vector(rel_bias):
    n_heads = rel_bias.shape[0]
    n_keys = KEY_BLOCKS * Q_BLOCK
    top = (KEY_BLOCKS - 1) * Q_BLOCK + Q_BLOCK - 1
    n_dist = Q_BLOCK + n_keys - 1
    vec = jnp.concatenate([
        jnp.broadcast_to(rel_bias[:, -1:], (n_heads, top - REL_CLIP)),
        rel_bias[:, ::-1],
        jnp.broadcast_to(rel_bias[:, :1], (n_heads, n_dist - (top - REL_CLIP) - rel_bias.shape[1])),
        jnp.zeros((n_heads, 1), rel_bias.dtype)], axis=1)
    return jnp.roll(vec, -(Q_BLOCK - 1), axis=1).astype(F32)[:, None, :]


@jax.jit
def _forward(x, c, ada_w, ada_b, norm_g, a_w_in, a_w_group, a_scale, a_w_out, kv_norm_g,
             kv_ada_w, kv_ada_b, w_kv, b_w_in, b_rel_bias, b_w_out, final_g):
    b, s, d = x.shape
    depth = ada_w.shape[0]
    n_a = a_w_in.shape[0]
    c_pad = jnp.pad(c, ((0, 8 - b), (0, 0)))
    mods = _ada_call(c_pad, ada_w, ada_b.reshape(depth, 1, 3 * d))[:, :b]
    kvm = _ada_call(c_pad, kv_ada_w[None], kv_ada_b.reshape(1, 1, 2 * d))[0, :b]

    def vec(a):
        return a.reshape(b, 1, d)

    a_w_in = _fold_call(a_w_in, a_w_group)
    a_w_out = a_w_out.astype(BF16)
    b_w_in, b_w_out = b_w_in.astype(BF16), b_w_out.astype(BF16)
    w_kt = w_kv[:, :D_INNER].T.astype(BF16)
    w_v = w_kv[:, D_INNER:].astype(BF16)

    h = x
    k = v = None
    for layer in range(depth):
        shift, scale, gate = (vec(mods[layer, :, j * d:(j + 1) * d]) for j in range(3))
        g = norm_g[layer].reshape(1, d)
        if layer < n_a:
            h = _a_layer_call(h, shift, scale, gate, g, a_w_in,
                              a_scale[layer].reshape(1, D_INNER), a_w_out, layer)
        else:
            bi = layer - n_a
            if layer == n_a:
                kv = (vec(kvm[:, :d]), vec(kvm[:, d:]), kv_norm_g.reshape(1, d), w_kt, w_v)
                q, z, k, v = _bproj_call(h, shift, scale, g, b_w_in, bi, kv)
            else:
                q, z = _bproj_call(h, shift, scale, g, b_w_in, bi)
            last = layer == depth - 1
            h = _attn_call(q, k, v, z, h, gate, _bias_vector(b_rel_bias[bi]), b_w_out, bi,
                           final_g.reshape(1, d) if last else None)
    return h


def kernel(x, c, ada_w, ada_b, norm_g, a_w_in, a_w_group, a_scale, a_w_out, kv_norm_g, kv_ada_w, kv_ada_b, w_kv, b_w_in, b_rel_bias, b_w_out, final_g):
    return _forward(x, c, ada_w, ada_b, norm_g, a_w_in, a_w_group, a_scale, a_w_out,
                    kv_norm_g, kv_ada_w, kv_ada_b, w_kv, b_w_in, b_rel_bias, b_w_out, final_g)
```

```python
import functools

import jax
import jax.numpy as jnp
from jax import lax
from jax.experimental import pallas as pl
from jax.experimental.pallas import tpu as pltpu

F32 = jnp.float32
BF16 = jnp.bfloat16

D_MODEL = 1024
D_INNER = 2048
N_HEADS = 16
HEAD_DIM = 128
CHUNK = 64
LEFT_CHUNKS = 8
REL_CLIP = 128
POOL_WINDOWS = (2, 4, 8, 16)
POOL_GROUP_W = D_INNER // len(POOL_WINDOWS)
POOL_HALO = 16
EPS = 1e-6
LOG2E = 1.4426950408889634
Q_PRESCALE = HEAD_DIM ** -0.5 * LOG2E
NEG_INF = float("-inf")

LANES = 128
Q_BLOCK = 256
KEY_BLOCKS = LEFT_CHUNKS * CHUNK // Q_BLOCK + 1
SOFTMAX_ROWS = 64
A_TILE = 512
A_STREAMS = 2
P_TILE = 512
P_STREAMS = 2
VMEM_LIMIT = 54 * 1024 * 1024


def _silu(x):
    return x * (1.0 / (1.0 + jnp.exp(-x)))


def _rms_scale(x):
    return lax.rsqrt(jnp.mean(x * x, axis=-1, keepdims=True) + EPS)


def _modulated(x, g, shift, scale):
    return ((x * _rms_scale(x)) * (g * (1.0 + scale)) + shift).astype(BF16)


def _resident(shape, layer=None):
    if layer is None:
        return pl.BlockSpec(shape, lambda *_: (0,) * len(shape), pipeline_mode=pl.Buffered(1))
    return pl.BlockSpec((None,) + tuple(shape[1:]),
                        lambda *_: (layer,) + (0,) * (len(shape) - 1),
                        pipeline_mode=pl.Buffered(1))


def _ada_kernel(c_ref, w_ref, b_ref, o_ref):
    ca = _silu(c_ref[...])
    o_ref[0] = jnp.dot(ca, w_ref[0], preferred_element_type=F32) + b_ref[0]


def _ada_call(c_pad, w, b):
    n_layers, d, n = w.shape
    tn = 1024
    return pl.pallas_call(
        _ada_kernel,
        out_shape=jax.ShapeDtypeStruct((n_layers, c_pad.shape[0], n), F32),
        grid=(n_layers, n // tn),
        in_specs=[
            pl.BlockSpec(c_pad.shape, lambda l, j: (0, 0)),
            pl.BlockSpec((1, d, tn), lambda l, j: (l, 0, j)),
            pl.BlockSpec((1, 1, tn), lambda l, j: (l, 0, j)),
        ],
        out_specs=pl.BlockSpec((1, c_pad.shape[0], tn), lambda l, j: (l, 0, j)),
        compiler_params=pltpu.CompilerParams(
            dimension_semantics=("arbitrary", "arbitrary")),
        name="ada_mod",
    )(c_pad, w, b)


def _fold_kernel(win_ref, wg_ref, o_ref, *, n_groups):
    j = pl.program_id(1)

    def split(x):
        hi = x.astype(BF16)
        return hi, (x - hi.astype(F32)).astype(BF16)

    @pl.when(j < n_groups)
    def _():
        a_hi, a_lo = split(win_ref[0])
        b_hi, b_lo = split(wg_ref[0, 0])
        prod = (jnp.dot(a_hi, b_hi, preferred_element_type=F32)
                + jnp.dot(a_lo, b_hi, preferred_element_type=F32)
                + jnp.dot(a_hi, b_lo, preferred_element_type=F32))
        o_ref[0] = prod.astype(BF16)

    @pl.when(j >= n_groups)
    def _():
        o_ref[0] = win_ref[0].astype(BF16)


def _fold_call(w_in, w_group):
    n_layers, d, n = w_in.shape
    n_groups, gw = w_group.shape[1], w_group.shape[2]
    return pl.pallas_call(
        functools.partial(_fold_kernel, n_groups=n_groups),
        out_shape=jax.ShapeDtypeStruct(w_in.shape, BF16),
        grid=(n_layers, n // gw),
        in_specs=[pl.BlockSpec((1, d, gw), lambda l, j: (l, 0, j)),
                  pl.BlockSpec((1, 1, gw, gw), lambda l, j: (l, jnp.minimum(j, n_groups - 1), 0, 0))],
        out_specs=pl.BlockSpec((1, d, gw), lambda l, j: (l, 0, j)),
        compiler_params=pltpu.CompilerParams(
            dimension_semantics=("arbitrary", "arbitrary")),
        name="fold_group_mix",
    )(w_in, w_group)


def _a_kernel(h_ref, sh_ref, sc_ref, gt_ref, g_ref, win_ref, asc_ref, wout_ref,
              o_ref, carry_ref, u_ref):
    n_streams, ts = h_ref.shape[0], h_ref.shape[1]
    streams = range(n_streams)
    si = pl.program_id(1)

    @pl.when(si == 0)
    def _():
        carry_ref[...] = jnp.zeros_like(carry_ref)

    rows = [slice(n * ts, (n + 1) * ts) for n in streams]
    for n in streams:
        u_ref[rows[n]] = _modulated(h_ref[n], g_ref[...], sh_ref[n], sc_ref[n])

    groups = sorted(range(len(POOL_WINDOWS)), key=lambda g: -POOL_WINDOWS[g])
    span = {g: slice(g * POOL_GROUP_W, (g + 1) * POOL_GROUP_W) for g in groups}
    val = {g: jnp.dot(u_ref[...], win_ref[:, span[g]], preferred_element_type=F32)
           for g in groups}
    t = si * ts + lax.broadcasted_iota(jnp.int32, (ts, LANES), 0)
    gz = {}
    for g in groups:
        w = POOL_WINDOWS[g]
        inv = 1.0 / jnp.minimum(t + 1, w).astype(F32)
        inv = jnp.concatenate([inv] * (POOL_GROUP_W // LANES), axis=1)
        z_all = jnp.dot(u_ref[...], win_ref[:, D_INNER + span[g].start:D_INNER + span[g].stop],
                        preferred_element_type=F32)
        for n in streams:
            z = z_all[rows[n]]
            v = val[g][rows[n]]
            s = jnp.concatenate([carry_ref[n, :, span[g]], v], axis=0)
            carry_ref[n, :, span[g]] = v[ts - POOL_HALO:, :]
            k = 1
            while k < w:
                s = s + pltpu.roll(s, k, axis=0)
                k *= 2
            mixed = s[POOL_HALO:, :] * inv - v
            gz[n, g] = ((mixed * asc_ref[:, span[g]]) * _silu(z)).astype(BF16)
    gated = jnp.concatenate(
        [jnp.concatenate([gz[n, g] for g in sorted(groups)], axis=1) for n in streams], axis=0)
    y = jnp.dot(gated, wout_ref[...], preferred_element_type=F32)
    for n in streams:
        o_ref[n] = h_ref[n] + gt_ref[n] * y[rows[n]]


def _a_layer_call(h, shift, scale, gate, g, w_in, a_scale, w_out, layer):
    b, s, d = h.shape
    ts, ns = A_TILE, A_STREAMS
    row = pl.BlockSpec((ns, ts, d), lambda bi, si: (bi, si, 0))
    vec = pl.BlockSpec((ns, 1, d), lambda bi, si: (bi, 0, 0))
    return pl.pallas_call(
        _a_kernel,
        out_shape=jax.ShapeDtypeStruct(h.shape, F32),
        grid=(b // ns, s // ts),
        in_specs=[row, vec, vec, vec,
                  _resident((1, d)), _resident(w_in.shape, layer),
                  _resident((1, D_INNER)), _resident(w_out.shape, layer)],
        out_specs=row,
        scratch_shapes=[pltpu.VMEM((ns, POOL_HALO, D_INNER), F32),
                        pltpu.VMEM((ns * ts, d), BF16)],
        compiler_params=pltpu.CompilerParams(
            dimension_semantics=("arbitrary", "arbitrary"), vmem_limit_bytes=VMEM_LIMIT),
        name="a_layer",
    )(h, shift, scale, gate, g, w_in, a_scale, w_out)


def _store_heads(dst_ref, n, val):
    for hd in range(N_HEADS):
        dst_ref[n, hd] = val[:, hd * HEAD_DIM:(hd + 1) * HEAD_DIM].astype(BF16)


def _bproj_kernel(*refs, with_kv):
    if with_kv:
        (h_ref, sh_ref, sc_ref, g_ref, win_ref, ksh_ref, ksc_ref, kg_ref, wkt_ref, wv_ref,
         q_ref, z_ref, kt_ref, v_ref) = refs
    else:
        h_ref, sh_ref, sc_ref, g_ref, win_ref, q_ref, z_ref = refs
    n_streams, ts = h_ref.shape[0], h_ref.shape[1]
    streams = range(n_streams)
    rows = [slice(n * ts, (n + 1) * ts) for n in streams]
    xn = [h_ref[n] * _rms_scale(h_ref[n]) for n in streams]
    u = jnp.concatenate([(xn[n] * (g_ref[...] * (1.0 + sc_ref[n])) + sh_ref[n]).astype(BF16)
                         for n in streams], axis=0)
    q = jnp.dot(u, win_ref[:, :D_INNER], preferred_element_type=F32)
    for n in streams:
        _store_heads(q_ref, n, q[rows[n]] * Q_PRESCALE)
    z = jnp.dot(u, win_ref[:, D_INNER:], preferred_element_type=F32)
    for n in streams:
        z_ref[n] = _silu(z[rows[n]])
    if with_kv:
        hk = jnp.concatenate([(xn[n] * (kg_ref[...] * (1.0 + ksc_ref[n])) + ksh_ref[n]).astype(BF16)
                              for n in streams], axis=0)
        v = jnp.dot(hk, wv_ref[...], preferred_element_type=F32)
        kt = lax.dot_general(wkt_ref[...], hk, (((1,), (1,)), ((), ())),
                             preferred_element_type=F32)
        for n in streams:
            _store_heads(v_ref, n, v[rows[n]])
            for hd in range(N_HEADS):
                kt_ref[n, hd] = kt[hd * HEAD_DIM:(hd + 1) * HEAD_DIM, rows[n]].astype(BF16)


def _bproj_call(h, shift, scale, g, w_in, layer, kv=None):
    b, s, d = h.shape
    ts = P_TILE
    ns = 1 if kv is not None else P_STREAMS
    row = pl.BlockSpec((ns, ts, d), lambda bi, si: (bi, si, 0))
    vec = pl.BlockSpec((ns, 1, d), lambda bi, si: (bi, 0, 0))
    heads = pl.BlockSpec((ns, N_HEADS, ts, HEAD_DIM), lambda bi, si: (bi, 0, si, 0))
    heads_shape = jax.ShapeDtypeStruct((b, N_HEADS, s, HEAD_DIM), BF16)
    in_specs = [row, vec, vec, _resident((1, d)), _resident(w_in.shape, layer)]
    args = [h, shift, scale, g, w_in]
    out_shape = [heads_shape, jax.ShapeDtypeStruct((b, s, D_INNER), F32)]
    out_specs = [heads, pl.BlockSpec((ns, ts, D_INNER), lambda bi, si: (bi, si, 0))]
    if kv is not None:
        kshift, kscale, kg, w_kt, w_v = kv
        in_specs += [vec, vec, _resident((1, d)), _resident(w_kt.shape), _resident(w_v.shape)]
        args += [kshift, kscale, kg, w_kt, w_v]
        out_shape += [jax.ShapeDtypeStruct((b, N_HEADS, HEAD_DIM, s), BF16), heads_shape]
        out_specs += [pl.BlockSpec((ns, N_HEADS, HEAD_DIM, ts), lambda bi, si: (bi, 0, 0, si)),
                      heads]
    return pl.pallas_call(
        functools.partial(_bproj_kernel, with_kv=kv is not None),
        out_shape=out_shape,
        grid=(b // ns, s // ts),
        in_specs=in_specs,
        out_specs=out_specs,
        compiler_params=pltpu.CompilerParams(
            dimension_semantics=("arbitrary", "arbitrary"), vmem_limit_bytes=VMEM_LIMIT),
        name="b_proj_kv" if kv is not None else "b_proj",
    )(*args)


def _attn_kernel(*refs, final):
    if final:
        (q_ref, k_ref, v_ref, z_ref, h_ref, gt_ref, bvec_ref, wout_ref, fg_ref,
         o_ref, kring, vring, att_ref, s_scr, p_scr, m_scr, tab_ref) = refs
    else:
        (q_ref, k_ref, v_ref, z_ref, h_ref, gt_ref, bvec_ref, wout_ref,
         o_ref, kring, vring, att_ref, s_scr, p_scr, m_scr, tab_ref) = refs
    i = pl.program_id(1)
    n_keys = KEY_BLOCKS * Q_BLOCK

    @pl.when((pl.program_id(0) == 0) & (i == 0))
    def _():
        def one_head(hd, carry):
            skew = pltpu.roll(jnp.broadcast_to(bvec_ref[hd], (Q_BLOCK, bvec_ref.shape[-1])),
                              0, 1, stride=1, stride_axis=0)
            qc = lax.broadcasted_iota(jnp.int32, (Q_BLOCK, n_keys), 0) // CHUNK
            kc = lax.broadcasted_iota(jnp.int32, (Q_BLOCK, n_keys), 1) // CHUNK
            bias = skew[:, :n_keys] * LOG2E
            bias = jnp.where(kc >= qc, bias, NEG_INF)
            tab_ref[hd] = jnp.where(kc <= qc + LEFT_CHUNKS, bias, NEG_INF)
            return carry

        lax.fori_loop(0, N_HEADS, one_head, 0)
        vring[:, :, :, HEAD_DIM:] = jnp.ones(vring.shape[:3] + (HEAD_DIM,), BF16)

    slots = [lax.rem(i + 1 + j, KEY_BLOCKS) for j in range(KEY_BLOCKS - 1)]
    ones = jnp.ones((Q_BLOCK, HEAD_DIM), BF16)

    def scores(hd, buf, live):
        qh = q_ref[0, hd]
        older = slots[KEY_BLOCKS - live:]
        keys_t = jnp.concatenate([kring[s, hd] for s in older] + [k_ref[0, hd]], axis=1)
        s = jnp.dot(qh, keys_t, preferred_element_type=F32)
        s = s + tab_ref[hd, :, (KEY_BLOCKS - live) * Q_BLOCK:]
        s_scr[buf, :, :live * Q_BLOCK] = s
        m_scr[buf] = functools.reduce(
            jnp.maximum, [s[:, c * LANES:(c + 1) * LANES] for c in range(s.shape[1] // LANES)])

    def softmax(buf, live):
        width = live * Q_BLOCK
        for r in range(0, Q_BLOCK, SOFTMAX_ROWS):
            rows = slice(r, r + SOFTMAX_ROWS)
            m = m_scr[buf, rows].max(axis=-1, keepdims=True)
            p_scr[buf, rows, :width] = jnp.exp2(s_scr[buf, rows, :width] - m).astype(BF16)

    def values(hd, buf, live):
        older = slots[KEY_BLOCKS - live:]
        own = jnp.concatenate([v_ref[0, hd], ones], axis=1)
        vals = jnp.concatenate([vring[s, hd] for s in older] + [own], axis=0)
        oe = jnp.dot(p_scr[buf, :, :live * Q_BLOCK], vals,
                     preferred_element_type=F32)
        att_ref[hd] = oe[:, :HEAD_DIM] * (1.0 / oe[:, HEAD_DIM:])

    def finish():
        kring[lax.rem(i, KEY_BLOCKS)] = k_ref[0]
        vring[lax.rem(i, KEY_BLOCKS), :, :, :HEAD_DIM] = v_ref[0]
        att = jnp.concatenate([att_ref[hd] for hd in range(N_HEADS)], axis=1)
        gz = att * z_ref[0]
        y = jnp.dot(gz.astype(BF16), wout_ref[...], preferred_element_type=F32)
        out = h_ref[0] + gt_ref[0] * y
        if final:
            out = out * _rms_scale(out) * fg_ref[...]
        o_ref[0] = out

    def all_heads(live):
        scores(0, 0, live)
        scores(1, 1, live)
        softmax(0, live)
        for hd in range(1, N_HEADS - 1):
            scores(hd + 1, (hd + 1) % 2, live)
            softmax(hd % 2, live)
            values(hd - 1, (hd - 1) % 2, live)
        softmax((N_HEADS - 1) % 2, live)
        values(N_HEADS - 2, (N_HEADS - 2) % 2, live)
        values(N_HEADS - 1, (N_HEADS - 1) % 2, live)
        finish()

    for live in range(1, KEY_BLOCKS):
        @pl.when(i == live - 1)
        def _(live=live):
            all_heads(live)

    @pl.when(i >= KEY_BLOCKS - 1)
    def _():
        all_heads(KEY_BLOCKS)


def _attn_call(q, k, v, z, h, gate, bias_vec, w_out, layer, final_g=None):
    b, s, d = h.shape
    tq = Q_BLOCK
    heads = pl.BlockSpec((1, N_HEADS, tq, HEAD_DIM), lambda bi, i: (bi, 0, i, 0))
    row = pl.BlockSpec((1, tq, d), lambda bi, i: (bi, i, 0))
    heads_t = pl.BlockSpec((1, N_HEADS, HEAD_DIM, tq), lambda bi, i: (bi, 0, 0, i))
    in_specs = [heads, heads_t, heads,
                pl.BlockSpec((1, tq, D_INNER), lambda bi, i: (bi, i, 0)),
                row,
                pl.BlockSpec((1, 1, d), lambda bi, i: (bi, 0, 0)),
                _resident(bias_vec.shape), _resident(w_out.shape, layer)]
    args = [q, k, v, z, h, gate, bias_vec, w_out]
    if final_g is not None:
        in_specs.append(_resident((1, d)))
        args.append(final_g)
    kring = pltpu.VMEM((KEY_BLOCKS, N_HEADS, HEAD_DIM, tq), BF16)
    vring = pltpu.VMEM((KEY_BLOCKS, N_HEADS, tq, 2 * HEAD_DIM), BF16)
    return pl.pallas_call(
        functools.partial(_attn_kernel, final=final_g is not None),
        out_shape=jax.ShapeDtypeStruct(h.shape, F32),
        grid=(b, s // tq),
        in_specs=in_specs,
        out_specs=row,
        scratch_shapes=[kring, vring, pltpu.VMEM((N_HEADS, tq, HEAD_DIM), F32),
                        pltpu.VMEM((2, tq, KEY_BLOCKS * tq), F32),
                        pltpu.VMEM((2, tq, KEY_BLOCKS * tq), BF16),
                        pltpu.VMEM((2, tq, LANES), F32),
                        pltpu.VMEM((N_HEADS, tq, KEY_BLOCKS * tq), F32)],
        compiler_params=pltpu.CompilerParams(
            dimension_semantics=("arbitrary", "arbitrary"), vmem_limit_bytes=VMEM_LIMIT),
        name="band_attn_final" if final_g is not None else "band_attn",
    )(*args)


def _bias_vector(rel_bias):
    n_heads = rel_bias.shape[0]
    n_keys = KEY_BLOCKS * Q_BLOCK
    top = (KEY_BLOCKS - 1) * Q_BLOCK + Q_BLOCK - 1
    n_dist = Q_BLOCK + n_keys - 1
    vec = jnp.concatenate([
        jnp.broadcast_to(rel_bias[:, -1:], (n_heads, top - REL_CLIP)),
        rel_bias[:, ::-1],
        jnp.broadcast_to(rel_bias[:, :1], (n_heads, n_dist - (top - REL_CLIP) - rel_bias.shape[1])),
        jnp.zeros((n_heads, 1), rel_bias.dtype)], axis=1)
    return jnp.roll(vec, -(Q_BLOCK - 1), axis=1).astype(F32)[:, None, :]


@jax.jit
def _forward(x, c, ada_w, ada_b, norm_g, a_w_in, a_w_group, a_scale, a_w_out, kv_norm_g,
             kv_ada_w, kv_ada_b, w_kv, b_w_in, b_rel_bias, b_w_out, final_g):
    b, s, d = x.shape
    depth = ada_w.shape[0]
    n_a = a_w_in.shape[0]
    c_pad = jnp.pad(c, ((0, 8 - b), (0, 0)))
    mods = _ada_call(c_pad, ada_w, ada_b.reshape(depth, 1, 3 * d))[:, :b]
    kvm = _ada_call(c_pad, kv_ada_w[None], kv_ada_b.reshape(1, 1, 2 * d))[0, :b]

    def vec(a):
        return a.reshape(b, 1, d)

    a_w_in = _fold_call(a_w_in, a_w_group)
    a_w_out = a_w_out.astype(BF16)
    b_w_in, b_w_out = b_w_in.astype(BF16), b_w_out.astype(BF16)
    w_kt = w_kv[:, :D_INNER].T.astype(BF16)
    w_v = w_kv[:, D_INNER:].astype(BF16)

    h = x
    k = v = None
    for layer in range(depth):
        shift, scale, gate = (vec(mods[layer, :, j * d:(j + 1) * d]) for j in range(3))
        g = norm_g[layer].reshape(1, d)
        if layer < n_a:
            h = _a_layer_call(h, shift, scale, gate, g, a_w_in,
                              a_scale[layer].reshape(1, D_INNER), a_w_out, layer)
        else:
            bi = layer - n_a
            if layer == n_a:
                kv = (vec(kvm[:, :d]), vec(kvm[:, d:]), kv_norm_g.reshape(1, d), w_kt, w_v)
                q, z, k, v = _bproj_call(h, shift, scale, g, b_w_in, bi, kv)
            else:
                q, z = _bproj_call(h, shift, scale, g, b_w_in, bi)
            last = layer == depth - 1
            h = _attn_call(q, k, v, z, h, gate, _bias_vector(b_rel_bias[bi]), b_w_out, bi,
                           final_g.reshape(1, d) if last else None)
    return h


def kernel(x, c, ada_w, ada_b, norm_g, a_w_in, a_w_group, a_scale, a_w_out, kv_norm_g, kv_ada_w, kv_ada_b, w_kv, b_w_in, b_rel_bias, b_w_out, final_g):
    return _forward(x, c, ada_w, ada_b, norm_g, a_w_in, a_w_group, a_scale, a_w_out,
                    kv_norm_g, kv_ada_w, kv_ada_b, w_kv, b_w_in, b_rel_bias, b_w_out, final_g)
```

```python
import functools

import jax
import jax.numpy as jnp
from jax import lax
from jax.experimental import pallas as pl
from jax.experimental.pallas import tpu as pltpu

F32 = jnp.float32
BF16 = jnp.bfloat16

D_MODEL = 1024
D_INNER = 2048
N_HEADS = 16
HEAD_DIM = 128
CHUNK = 64
LEFT_CHUNKS = 8
REL_CLIP = 128
POOL_WINDOWS = (2, 4, 8, 16)
POOL_GROUP_W = D_INNER // len(POOL_WINDOWS)
POOL_HALO = 16
EPS = 1e-6
LOG2E = 1.4426950408889634
Q_PRESCALE = HEAD_DIM ** -0.5 * LOG2E
NEG_INF = float("-inf")

LANES = 128
Q_BLOCK = 256
KEY_BLOCKS = LEFT_CHUNKS * CHUNK // Q_BLOCK + 1
SOFTMAX_ROWS = 64
A_TILE = 512
A_STREAMS = 2
P_TILE = 512
P_STREAMS = 2
VMEM_LIMIT = 54 * 1024 * 1024


def _silu(x):
    return x * (1.0 / (1.0 + jnp.exp(-x)))


def _rms_scale(x):
    return lax.rsqrt(jnp.mean(x * x, axis=-1, keepdims=True) + EPS)


def _modulated(x, g, shift, scale):
    return ((x * _rms_scale(x)) * (g * (1.0 + scale)) + shift).astype(BF16)


def _resident(shape, layer=None):
    if layer is None:
        return pl.BlockSpec(shape, lambda *_: (0,) * len(shape), pipeline_mode=pl.Buffered(1))
    return pl.BlockSpec((None,) + tuple(shape[1:]),
                        lambda *_: (layer,) + (0,) * (len(shape) - 1),
                        pipeline_mode=pl.Buffered(1))


def _ada_kernel(c_ref, w_ref, b_ref, o_ref):
    ca = _silu(c_ref[...])
    o_ref[0] = jnp.dot(ca, w_ref[0], preferred_element_type=F32) + b_ref[0]


def _ada_call(c_pad, w, b):
    n_layers, d, n = w.shape
    return pl.pallas_call(
        _ada_kernel,
        out_shape=jax.ShapeDtypeStruct((n_layers, c_pad.shape[0], n), F32),
        grid=(n_layers,),
        in_specs=[
            pl.BlockSpec(c_pad.shape, lambda l: (0, 0)),
            pl.BlockSpec((1, d, n), lambda l: (l, 0, 0)),
            pl.BlockSpec((1, 1, n), lambda l: (l, 0, 0)),
        ],
        out_specs=pl.BlockSpec((1, c_pad.shape[0], n), lambda l: (l, 0, 0)),
        compiler_params=pltpu.CompilerParams(
            dimension_semantics=("arbitrary",), vmem_limit_bytes=VMEM_LIMIT),
        name="ada_mod",
    )(c_pad, w, b)


def _fold_kernel(win_ref, wg_ref, o_ref, *, n_groups):
    j = pl.program_id(1)

    def split(x):
        hi = x.astype(BF16)
        return hi, (x - hi.astype(F32)).astype(BF16)

    @pl.when(j < n_groups)
    def _():
        a_hi, a_lo = split(win_ref[0])
        b_hi, b_lo = split(wg_ref[0, 0])
        prod = (jnp.dot(a_hi, b_hi, preferred_element_type=F32)
                + jnp.dot(a_lo, b_hi, preferred_element_type=F32)
                + jnp.dot(a_hi, b_lo, preferred_element_type=F32))
        o_ref[0] = prod.astype(BF16)

    @pl.when(j >= n_groups)
    def _():
        o_ref[0] = win_ref[0].astype(BF16)


def _fold_call(w_in, w_group):
    n_layers, d, n = w_in.shape
    n_groups, gw = w_group.shape[1], w_group.shape[2]
    return pl.pallas_call(
        functools.partial(_fold_kernel, n_groups=n_groups),
        out_shape=jax.ShapeDtypeStruct(w_in.shape, BF16),
        grid=(n_layers, n // gw),
        in_specs=[pl.BlockSpec((1, d, gw), lambda l, j: (l, 0, j)),
                  pl.BlockSpec((1, 1, gw, gw), lambda l, j: (l, jnp.minimum(j, n_groups - 1), 0, 0))],
        out_specs=pl.BlockSpec((1, d, gw), lambda l, j: (l, 0, j)),
        compiler_params=pltpu.CompilerParams(
            dimension_semantics=("arbitrary", "arbitrary")),
        name="fold_group_mix",
    )(w_in, w_group)


def _a_kernel(h_ref, sh_ref, sc_ref, gt_ref, g_ref, win_ref, asc_ref, wout_ref,
              o_ref, carry_ref, u_ref):
    n_streams, ts = h_ref.shape[0], h_ref.shape[1]
    streams = range(n_streams)
    si = pl.program_id(1)

    @pl.when(si == 0)
    def _():
        carry_ref[...] = jnp.zeros_like(carry_ref)

    rows = [slice(n * ts, (n + 1) * ts) for n in streams]
    for n in streams:
        u_ref[rows[n]] = _modulated(h_ref[n], g_ref[...], sh_ref[n], sc_ref[n])

    groups = sorted(range(len(POOL_WINDOWS)), key=lambda g: -POOL_WINDOWS[g])
    span = {g: slice(g * POOL_GROUP_W, (g + 1) * POOL_GROUP_W) for g in groups}
    val = {g: jnp.dot(u_ref[...], win_ref[:, span[g]], preferred_element_type=F32)
           for g in groups}
    t = si * ts + lax.broadcasted_iota(jnp.int32, (ts, LANES), 0)
    gz = {}
    for g in groups:
        w = POOL_WINDOWS[g]
        inv = 1.0 / jnp.minimum(t + 1, w).astype(F32)
        inv = jnp.concatenate([inv] * (POOL_GROUP_W // LANES), axis=1)
        z_all = jnp.dot(u_ref[...], win_ref[:, D_INNER + span[g].start:D_INNER + span[g].stop],
                        preferred_element_type=F32)
        for n in streams:
            z = z_all[rows[n]]
            v = val[g][rows[n]]
            s = jnp.concatenate([carry_ref[n, :, span[g]], v], axis=0)
            carry_ref[n, :, span[g]] = v[ts - POOL_HALO:, :]
            k = 1
            while k < w:
                s = s + pltpu.roll(s, k, axis=0)
                k *= 2
            mixed = s[POOL_HALO:, :] * inv - v
            gz[n, g] = ((mixed * asc_ref[:, span[g]]) * _silu(z)).astype(BF16)
    gated = jnp.concatenate(
        [jnp.concatenate([gz[n, g] for g in sorted(groups)], axis=1) for n in streams], axis=0)
    y = jnp.dot(gated, wout_ref[...], preferred_element_type=F32)
    for n in streams:
        o_ref[n] = h_ref[n] + gt_ref[n] * y[rows[n]]


def _a_layer_call(h, shift, scale, gate, g, w_in, a_scale, w_out, layer):
    b, s, d = h.shape
    ts, ns = A_TILE, A_STREAMS
    assert b % ns == 0 and s % ts == 0
    row = pl.BlockSpec((ns, ts, d), lambda bi, si: (bi, si, 0))
    vec = pl.BlockSpec((ns, 1, d), lambda bi, si: (bi, 0, 0))
    return pl.pallas_call(
        _a_kernel,
        out_shape=jax.ShapeDtypeStruct(h.shape, F32),
        grid=(b // ns, s // ts),
        in_specs=[row, vec, vec, vec,
                  _resident((1, d)), _resident(w_in.shape, layer),
                  _resident((1, D_INNER)), _resident(w_out.shape, layer)],
        out_specs=row,
        scratch_shapes=[pltpu.VMEM((ns, POOL_HALO, D_INNER), F32),
                        pltpu.VMEM((ns * ts, d), BF16)],
        compiler_params=pltpu.CompilerParams(
            dimension_semantics=("arbitrary", "arbitrary"), vmem_limit_bytes=VMEM_LIMIT),
        name="a_layer",
    )(h, shift, scale, gate, g, w_in, a_scale, w_out)


def _store_heads(dst_ref, n, val):
    for hd in range(N_HEADS):
        dst_ref[n, hd] = val[:, hd * HEAD_DIM:(hd + 1) * HEAD_DIM].astype(BF16)


def _bproj_kernel(*refs, with_kv):
    if with_kv:
        (h_ref, sh_ref, sc_ref, g_ref, win_ref, ksh_ref, ksc_ref, kg_ref, wkt_ref, wv_ref,
         q_ref, z_ref, kt_ref, v_ref) = refs
    else:
        h_ref, sh_ref, sc_ref, g_ref, win_ref, q_ref, z_ref = refs
    n_streams, ts = h_ref.shape[0], h_ref.shape[1]
    streams = range(n_streams)
    rows = [slice(n * ts, (n + 1) * ts) for n in streams]
    xn = [h_ref[n] * _rms_scale(h_ref[n]) for n in streams]
    u = jnp.concatenate([(xn[n] * (g_ref[...] * (1.0 + sc_ref[n])) + sh_ref[n]).astype(BF16)
                         for n in streams], axis=0)
    q = jnp.dot(u, win_ref[:, :D_INNER], preferred_element_type=F32)
    for n in streams:
        _store_heads(q_ref, n, q[rows[n]] * Q_PRESCALE)
    z = jnp.dot(u, win_ref[:, D_INNER:], preferred_element_type=F32)
    for n in streams:
        z_ref[n] = z[rows[n]]
    if with_kv:
        hk = jnp.concatenate([(xn[n] * (kg_ref[...] * (1.0 + ksc_ref[n])) + ksh_ref[n]).astype(BF16)
                              for n in streams], axis=0)
        v = jnp.dot(hk, wv_ref[...], preferred_element_type=F32)
        kt = lax.dot_general(wkt_ref[...], hk, (((1,), (1,)), ((), ())),
                             preferred_element_type=F32)
        for n in streams:
            _store_heads(v_ref, n, v[rows[n]])
            for hd in range(N_HEADS):
                kt_ref[n, hd] = kt[hd * HEAD_DIM:(hd + 1) * HEAD_DIM, rows[n]].astype(BF16)


def _bproj_call(h, shift, scale, g, w_in, layer, kv=None):
    b, s, d = h.shape
    ts = P_TILE
    ns = 1 if kv is not None else P_STREAMS
    assert b % ns == 0 and s % ts == 0
    row = pl.BlockSpec((ns, ts, d), lambda bi, si: (bi, si, 0))
    vec = pl.BlockSpec((ns, 1, d), lambda bi, si: (bi, 0, 0))
    heads = pl.BlockSpec((ns, N_HEADS, ts, HEAD_DIM), lambda bi, si: (bi, 0, si, 0))
    heads_shape = jax.ShapeDtypeStruct((b, N_HEADS, s, HEAD_DIM), BF16)
    in_specs = [row, vec, vec, _resident((1, d)), _resident(w_in.shape, layer)]
    args = [h, shift, scale, g, w_in]
    out_shape = [heads_shape, jax.ShapeDtypeStruct((b, s, D_INNER), F32)]
    out_specs = [heads, pl.BlockSpec((ns, ts, D_INNER), lambda bi, si: (bi, si, 0))]
    if kv is not None:
        kshift, kscale, kg, w_kt, w_v = kv
        in_specs += [vec, vec, _resident((1, d)), _resident(w_kt.shape), _resident(w_v.shape)]
        args += [kshift, kscale, kg, w_kt, w_v]
        out_shape += [jax.ShapeDtypeStruct((b, N_HEADS, HEAD_DIM, s), BF16), heads_shape]
        out_specs += [pl.BlockSpec((ns, N_HEADS, HEAD_DIM, ts), lambda bi, si: (bi, 0, 0, si)),
                      heads]
    return pl.pallas_call(
        functools.partial(_bproj_kernel, with_kv=kv is not None),
        out_shape=out_shape,
        grid=(b // ns, s // ts),
        in_specs=in_specs,
        out_specs=out_specs,
        compiler_params=pltpu.CompilerParams(
            dimension_semantics=("arbitrary", "arbitrary"), vmem_limit_bytes=VMEM_LIMIT),
        name="b_proj_kv" if kv is not None else "b_proj",
    )(*args)


def _attn_kernel(*refs, final):
    if final:
        (q_ref, k_ref, v_ref, z_ref, h_ref, gt_ref, bvec_ref, wout_ref, fg_ref,
         o_ref, kring, vring, att_ref, s_scr, p_scr, m_scr, tab_ref) = refs
    else:
        (q_ref, k_ref, v_ref, z_ref, h_ref, gt_ref, bvec_ref, wout_ref,
         o_ref, kring, vring, att_ref, s_scr, p_scr, m_scr, tab_ref) = refs
    i = pl.program_id(1)
    n_keys = KEY_BLOCKS * Q_BLOCK

    @pl.when((pl.program_id(0) == 0) & (i == 0))
    def _():
        def one_head(hd, carry):
            skew = pltpu.roll(jnp.broadcast_to(bvec_ref[hd], (Q_BLOCK, bvec_ref.shape[-1])),
                              0, 1, stride=1, stride_axis=0)
            qc = lax.broadcasted_iota(jnp.int32, (Q_BLOCK, n_keys), 0) // CHUNK
            kc = lax.broadcasted_iota(jnp.int32, (Q_BLOCK, n_keys), 1) // CHUNK
            bias = skew[:, :n_keys] * LOG2E
            bias = jnp.where(kc >= qc, bias, NEG_INF)
            tab_ref[hd] = jnp.where(kc <= qc + LEFT_CHUNKS, bias, NEG_INF)
            return carry

        lax.fori_loop(0, N_HEADS, one_head, 0)
        vring[:, :, :, HEAD_DIM:] = jnp.ones(vring.shape[:3] + (HEAD_DIM,), BF16)

    slots = [lax.rem(i + 1 + j, KEY_BLOCKS) for j in range(KEY_BLOCKS - 1)]
    ones = jnp.ones((Q_BLOCK, HEAD_DIM), BF16)

    def scores(hd, buf, live):
        qh = q_ref[0, hd]
        older = slots[KEY_BLOCKS - live:]
        keys_t = jnp.concatenate([kring[s, hd] for s in older] + [k_ref[0, hd]], axis=1)
        s = jnp.dot(qh, keys_t, preferred_element_type=F32)
        s = s + tab_ref[hd, :, (KEY_BLOCKS - live) * Q_BLOCK:]
        s_scr[buf, :, :live * Q_BLOCK] = s
        m_scr[buf] = functools.reduce(
            jnp.maximum, [s[:, c * LANES:(c + 1) * LANES] for c in range(s.shape[1] // LANES)])

    def softmax(buf, live):
        width = live * Q_BLOCK
        for r in range(0, Q_BLOCK, SOFTMAX_ROWS):
            rows = slice(r, r + SOFTMAX_ROWS)
            m = m_scr[buf, rows].max(axis=-1, keepdims=True)
            p_scr[buf, rows, :width] = jnp.exp2(s_scr[buf, rows, :width] - m).astype(BF16)

    def values(hd, buf, live):
        older = slots[KEY_BLOCKS - live:]
        own = jnp.concatenate([v_ref[0, hd], ones], axis=1)
        vals = jnp.concatenate([vring[s, hd] for s in older] + [own], axis=0)
        oe = jnp.dot(p_scr[buf, :, :live * Q_BLOCK], vals,
                     preferred_element_type=F32)
        att_ref[hd] = oe[:, :HEAD_DIM] * (1.0 / oe[:, HEAD_DIM:])

    def finish():
        kring[lax.rem(i, KEY_BLOCKS)] = k_ref[0]
        vring[lax.rem(i, KEY_BLOCKS), :, :, :HEAD_DIM] = v_ref[0]
        att = jnp.concatenate([att_ref[hd] for hd in range(N_HEADS)], axis=1)
        gz = att * _silu(z_ref[0])
        y = lax.dot_general(wout_ref[...], gz.astype(BF16), (((1,), (1,)), ((), ())),
                            preferred_element_type=F32).T
        out = h_ref[0] + gt_ref[0] * y
        if final:
            out = out * _rms_scale(out) * fg_ref[...]
        o_ref[0] = out

    def all_heads(live):
        scores(0, 0, live)
        scores(1, 1, live)
        softmax(0, live)
        for hd in range(1, N_HEADS - 1):
            scores(hd + 1, (hd + 1) % 2, live)
            softmax(hd % 2, live)
            values(hd - 1, (hd - 1) % 2, live)
        softmax((N_HEADS - 1) % 2, live)
        values(N_HEADS - 2, (N_HEADS - 2) % 2, live)
        values(N_HEADS - 1, (N_HEADS - 1) % 2, live)
        finish()

    for live in range(1, KEY_BLOCKS):
        @pl.when(i == live - 1)
        def _(live=live):
            all_heads(live)

    @pl.when(i >= KEY_BLOCKS - 1)
    def _():
        all_heads(KEY_BLOCKS)


def _attn_call(q, k, v, z, h, gate, bias_vec, w_out, layer, final_g=None):
    b, s, d = h.shape
    tq = Q_BLOCK
    heads = pl.BlockSpec((1, N_HEADS, tq, HEAD_DIM), lambda bi, i: (bi, 0, i, 0))
    row = pl.BlockSpec((1, tq, d), lambda bi, i: (bi, i, 0))
    heads_t = pl.BlockSpec((1, N_HEADS, HEAD_DIM, tq), lambda bi, i: (bi, 0, 0, i))
    in_specs = [heads, heads_t, heads,
                pl.BlockSpec((1, tq, D_INNER), lambda bi, i: (bi, i, 0)),
                row,
                pl.BlockSpec((1, 1, d), lambda bi, i: (bi, 0, 0)),
                _resident(bias_vec.shape), _resident(w_out.shape, layer)]
    args = [q, k, v, z, h, gate, bias_vec, w_out]
    if final_g is not None:
        in_specs.append(_resident((1, d)))
        args.append(final_g)
    kring = pltpu.VMEM((KEY_BLOCKS, N_HEADS, HEAD_DIM, tq), BF16)
    vring = pltpu.VMEM((KEY_BLOCKS, N_HEADS, tq, 2 * HEAD_DIM), BF16)
    return pl.pallas_call(
        functools.partial(_attn_kernel, final=final_g is not None),
        out_shape=jax.ShapeDtypeStruct(h.shape, F32),
        grid=(b, s // tq),
        in_specs=in_specs,
        out_specs=row,
        scratch_shapes=[kring, vring, pltpu.VMEM((N_HEADS, tq, HEAD_DIM), F32),
                        pltpu.VMEM((2, tq, KEY_BLOCKS * tq), F32),
                        pltpu.VMEM((2, tq, KEY_BLOCKS * tq), BF16),
                        pltpu.VMEM((2, tq, LANES), F32),
                        pltpu.VMEM((N_HEADS, tq, KEY_BLOCKS * tq), F32)],
        compiler_params=pltpu.CompilerParams(
            dimension_semantics=("arbitrary", "arbitrary"), vmem_limit_bytes=VMEM_LIMIT),
        name="band_attn_final" if final_g is not None else "band_attn",
    )(*args)


def _bias_vector(rel_bias):
    n_heads = rel_bias.shape[0]
    n_keys = KEY_BLOCKS * Q_BLOCK
    top = (KEY_BLOCKS - 1) * Q_BLOCK + Q_BLOCK - 1
    n_dist = Q_BLOCK + n_keys - 1
    vec = jnp.concatenate([
        jnp.broadcast_to(rel_bias[:, -1:], (n_heads, top - REL_CLIP)),
        rel_bias[:, ::-1],
        jnp.broadcast_to(rel_bias[:, :1], (n_heads, n_dist - (top - REL_CLIP) - rel_bias.shape[1])),
        jnp.zeros((n_heads, 1), rel_bias.dtype)], axis=1)
    return jnp.roll(vec, -(Q_BLOCK - 1), axis=1).astype(F32)[:, None, :]


@jax.jit
def _forward(x, c, ada_w, ada_b, norm_g, a_w_in, a_w_group, a_scale, a_w_out, kv_norm_g,
             kv_ada_w, kv_ada_b, w_kv, b_w_in, b_rel_bias, b_w_out, final_g):
    b, s, d = x.shape
    depth = ada_w.shape[0]
    n_a = a_w_in.shape[0]
    assert depth > n_a
    c_pad = jnp.pad(c, ((0, 8 - b), (0, 0)))
    mods = _ada_call(c_pad, ada_w, ada_b.reshape(depth, 1, 3 * d))[:, :b]
    kvm = _ada_call(c_pad, kv_ada_w[None], kv_ada_b.reshape(1, 1, 2 * d))[0, :b]

    def vec(a):
        return a.reshape(b, 1, d)

    a_w_in = _fold_call(a_w_in, a_w_group)
    a_w_out = a_w_out.astype(BF16)
    b_w_in = b_w_in.astype(BF16)
    b_w_out_t = jnp.swapaxes(b_w_out, 1, 2).astype(BF16)
    w_kt = w_kv[:, :D_INNER].T.astype(BF16)
    w_v = w_kv[:, D_INNER:].astype(BF16)

    h = x
    k = v = None
    for layer in range(depth):
        shift, scale, gate = (vec(mods[layer, :, j * d:(j + 1) * d]) for j in range(3))
        g = norm_g[layer].reshape(1, d)
        if layer < n_a:
            h = _a_layer_call(h, shift, scale, gate, g, a_w_in,
                              a_scale[layer].reshape(1, D_INNER), a_w_out, layer)
        else:
            bi = layer - n_a
            if layer == n_a:
                kv = (vec(kvm[:, :d]), vec(kvm[:, d:]), kv_norm_g.reshape(1, d), w_kt, w_v)
                q, z, k, v = _bproj_call(h, shift, scale, g, b_w_in, bi, kv)
            else:
                q, z = _bproj_call(h, shift, scale, g, b_w_in, bi)
            last = layer == depth - 1
            h = _attn_call(q, k, v, z, h, gate, _bias_vector(b_rel_bias[bi]), b_w_out_t, bi,
                           final_g.reshape(1, d) if last else None)
    return h


def kernel(x, c, ada_w, ada_b, norm_g, a_w_in, a_w_group, a_scale, a_w_out, kv_norm_g, kv_ada_w, kv_ada_b, w_kv, b_w_in, b_rel_bias, b_w_out, final_g):
    return _forward(x, c, ada_w, ada_b, norm_g, a_w_in, a_w_group, a_scale, a_w_out,
                    kv_norm_g, kv_ada_w, kv_ada_b, w_kv, b_w_in, b_rel_bias, b_w_out, final_g)
```

```python
import functools

import jax
import jax.numpy as jnp
from jax import lax
from jax.experimental import pallas as pl
from jax.experimental.pallas import tpu as pltpu

F32 = jnp.float32
BF16 = jnp.bfloat16

D_MODEL = 1024
D_INNER = 2048
N_HEADS = 16
HEAD_DIM = 128
CHUNK = 64
LEFT_CHUNKS = 8
REL_CLIP = 128
POOL_WINDOWS = (2, 4, 8, 16)
POOL_GROUP_W = D_INNER // len(POOL_WINDOWS)
POOL_HALO = 16
EPS = 1e-6
LOG2E = 1.4426950408889634
Q_PRESCALE = HEAD_DIM ** -0.5 * LOG2E
NEG_INF = float("-inf")

LANES = 128
Q_BLOCK = 256
KEY_BLOCKS = LEFT_CHUNKS * CHUNK // Q_BLOCK + 1
SOFTMAX_ROWS = 64
A_TILE = 512
A_STREAMS = 2
P_TILE = 512
P_STREAMS = 2
VMEM_LIMIT = 54 * 1024 * 1024
VEC_ROWS = 4


def _silu(x):
    return x * (1.0 / (1.0 + jnp.exp(-x)))


def _rms_scale(x):
    return lax.rsqrt(jnp.mean(x * x, axis=-1, keepdims=True) + EPS)


def _modulated(x, g, shift, scale):
    return ((x * _rms_scale(x)) * (g * (1.0 + scale)) + shift).astype(BF16)


def _resident(shape, layer=None):
    if layer is None:
        return pl.BlockSpec(shape, lambda *_: (0,) * len(shape), pipeline_mode=pl.Buffered(1))
    return pl.BlockSpec((None,) + tuple(shape[1:]),
                        lambda *_: (layer,) + (0,) * (len(shape) - 1),
                        pipeline_mode=pl.Buffered(1))


def _ada_kernel(c_ref, w_ref, b_ref, o_ref):
    ca = _silu(c_ref[...])
    o_ref[0] = jnp.dot(ca, w_ref[0], preferred_element_type=F32) + b_ref[0]


def _ada_call(c_pad, w, b):
    n_layers, d, n = w.shape
    return pl.pallas_call(
        _ada_kernel,
        out_shape=jax.ShapeDtypeStruct((n_layers, c_pad.shape[0], n), F32),
        grid=(n_layers,),
        in_specs=[
            pl.BlockSpec(c_pad.shape, lambda l: (0, 0)),
            pl.BlockSpec((1, d, n), lambda l: (l, 0, 0)),
            pl.BlockSpec((1, 1, n), lambda l: (l, 0, 0)),
        ],
        out_specs=pl.BlockSpec((1, c_pad.shape[0], n), lambda l: (l, 0, 0)),
        compiler_params=pltpu.CompilerParams(
            dimension_semantics=("arbitrary",), vmem_limit_bytes=VMEM_LIMIT),
        name="ada_mod",
    )(c_pad, w, b)


def _fold_kernel(win_ref, wg_ref, o_ref, *, n_groups):
    j = pl.program_id(1)

    def split(x):
        hi = x.astype(BF16)
        return hi, (x - hi.astype(F32)).astype(BF16)

    @pl.when(j < n_groups)
    def _():
        a_hi, a_lo = split(win_ref[0])
        b_hi, b_lo = split(wg_ref[0, 0])
        prod = (jnp.dot(a_hi, b_hi, preferred_element_type=F32)
                + jnp.dot(a_lo, b_hi, preferred_element_type=F32)
                + jnp.dot(a_hi, b_lo, preferred_element_type=F32))
        o_ref[0] = prod.astype(BF16)

    @pl.when(j >= n_groups)
    def _():
        o_ref[0] = win_ref[0].astype(BF16)


def _fold_call(w_in, w_group):
    n_layers, d, n = w_in.shape
    n_groups, gw = w_group.shape[1], w_group.shape[2]
    return pl.pallas_call(
        functools.partial(_fold_kernel, n_groups=n_groups),
        out_shape=jax.ShapeDtypeStruct(w_in.shape, BF16),
        grid=(n_layers, n // gw),
        in_specs=[pl.BlockSpec((1, d, gw), lambda l, j: (l, 0, j)),
                  pl.BlockSpec((1, 1, gw, gw), lambda l, j: (l, jnp.minimum(j, n_groups - 1), 0, 0))],
        out_specs=pl.BlockSpec((1, d, gw), lambda l, j: (l, 0, j)),
        compiler_params=pltpu.CompilerParams(
            dimension_semantics=("arbitrary", "arbitrary")),
        name="fold_group_mix",
    )(w_in, w_group)


def _a_kernel(h_ref, sh_ref, sc_ref, gt_ref, g_ref, win_ref, asc_ref, wout_ref,
              o_ref, carry_ref, u_ref):
    n_streams, ts = h_ref.shape[0], h_ref.shape[1]
    streams = range(n_streams)
    si = pl.program_id(1)

    @pl.when(si == 0)
    def _():
        carry_ref[...] = jnp.zeros_like(carry_ref)

    rows = [slice(n * ts, (n + 1) * ts) for n in streams]
    for n in streams:
        u_ref[rows[n]] = _modulated(h_ref[n], g_ref[0:1], sh_ref[n, 0:1], sc_ref[n, 0:1])

    groups = sorted(range(len(POOL_WINDOWS)), key=lambda g: -POOL_WINDOWS[g])
    span = {g: slice(g * POOL_GROUP_W, (g + 1) * POOL_GROUP_W) for g in groups}
    val = {g: jnp.dot(u_ref[...], win_ref[:, span[g]], preferred_element_type=F32)
           for g in groups}
    t = si * ts + lax.broadcasted_iota(jnp.int32, (ts, LANES), 0)
    gz = {}
    for g in groups:
        w = POOL_WINDOWS[g]
        inv = 1.0 / jnp.minimum(t + 1, w).astype(F32)
        inv = jnp.concatenate([inv] * (POOL_GROUP_W // LANES), axis=1)
        z_all = jnp.dot(u_ref[...], win_ref[:, D_INNER + span[g].start:D_INNER + span[g].stop],
                        preferred_element_type=F32)
        for n in streams:
            z = z_all[rows[n]]
            v = val[g][rows[n]]
            s = jnp.concatenate([carry_ref[n, :, span[g]], v], axis=0)
            carry_ref[n, :, span[g]] = v[ts - POOL_HALO:, :]
            k = 1
            while k < w:
                s = s + pltpu.roll(s, k, axis=0)
                k *= 2
            mixed = s[POOL_HALO:, :] * inv - v
            gz[n, g] = ((mixed * asc_ref[0:1, span[g]]) * _silu(z)).astype(BF16)
    gated = jnp.concatenate(
        [jnp.concatenate([gz[n, g] for g in sorted(groups)], axis=1) for n in streams], axis=0)
    y = jnp.dot(gated, wout_ref[...], preferred_element_type=F32)
    for n in streams:
        o_ref[n] = h_ref[n] + gt_ref[n, 0:1] * y[rows[n]]


def _a_layer_call(h, shift, scale, gate, g, w_in, a_scale, w_out, layer):
    b, s, d = h.shape
    ts, ns = A_TILE, A_STREAMS
    assert b % ns == 0 and s % ts == 0
    row = pl.BlockSpec((ns, ts, d), lambda bi, si: (bi, si, 0))
    vec = pl.BlockSpec((ns, VEC_ROWS, d), lambda bi, si: (bi, 0, 0))
    return pl.pallas_call(
        _a_kernel,
        out_shape=jax.ShapeDtypeStruct(h.shape, F32),
        grid=(b // ns, s // ts),
        in_specs=[row, vec, vec, vec,
                  _resident((VEC_ROWS, d)), _resident(w_in.shape, layer),
                  _resident((VEC_ROWS, D_INNER)), _resident(w_out.shape, layer)],
        out_specs=row,
        scratch_shapes=[pltpu.VMEM((ns, POOL_HALO, D_INNER), F32),
                        pltpu.VMEM((ns * ts, d), BF16)],
        compiler_params=pltpu.CompilerParams(
            dimension_semantics=("arbitrary", "arbitrary"), vmem_limit_bytes=VMEM_LIMIT),
        name="a_layer",
    )(h, shift, scale, gate, g, w_in, a_scale, w_out)


def _store_heads(dst_ref, n, val):
    for hd in range(N_HEADS):
        dst_ref[n, hd] = val[:, hd * HEAD_DIM:(hd + 1) * HEAD_DIM].astype(BF16)


def _bproj_kernel(*refs, with_kv):
    if with_kv:
        (h_ref, sh_ref, sc_ref, g_ref, win_ref, ksh_ref, ksc_ref, kg_ref, wkt_ref, wv_ref,
         q_ref, z_ref, kt_ref, v_ref) = refs
    else:
        h_ref, sh_ref, sc_ref, g_ref, win_ref, q_ref, z_ref = refs
    n_streams, ts = h_ref.shape[0], h_ref.shape[1]
    streams = range(n_streams)
    rows = [slice(n * ts, (n + 1) * ts) for n in streams]
    xn = [h_ref[n] * _rms_scale(h_ref[n]) for n in streams]
    u = jnp.concatenate([(xn[n] * (g_ref[0:1] * (1.0 + sc_ref[n, 0:1])) + sh_ref[n, 0:1]).astype(BF16)
                         for n in streams], axis=0)
    q = jnp.dot(u, win_ref[:, :D_INNER], preferred_element_type=F32)
    for n in streams:
        _store_heads(q_ref, n, q[rows[n]] * Q_PRESCALE)
    z = jnp.dot(u, win_ref[:, D_INNER:], preferred_element_type=F32)
    for n in streams:
        z_ref[n] = z[rows[n]]
    if with_kv:
        hk = jnp.concatenate([(xn[n] * (kg_ref[0:1] * (1.0 + ksc_ref[n, 0:1])) + ksh_ref[n, 0:1]).astype(BF16)
                              for n in streams], axis=0)
        v = jnp.dot(hk, wv_ref[...], preferred_element_type=F32)
        kt = lax.dot_general(wkt_ref[...], hk, (((1,), (1,)), ((), ())),
                             preferred_element_type=F32)
        for n in streams:
            _store_heads(v_ref, n, v[rows[n]])
            for hd in range(N_HEADS):
                kt_ref[n, hd] = kt[hd * HEAD_DIM:(hd + 1) * HEAD_DIM, rows[n]].astype(BF16)


def _bproj_call(h, shift, scale, g, w_in, layer, kv=None):
    b, s, d = h.shape
    ts = P_TILE
    ns = 1 if kv is not None else P_STREAMS
    assert b % ns == 0 and s % ts == 0
    row = pl.BlockSpec((ns, ts, d), lambda bi, si: (bi, si, 0))
    vec = pl.BlockSpec((ns, VEC_ROWS, d), lambda bi, si: (bi, 0, 0))
    heads = pl.BlockSpec((ns, N_HEADS, ts, HEAD_DIM), lambda bi, si: (bi, 0, si, 0))
    heads_shape = jax.ShapeDtypeStruct((b, N_HEADS, s, HEAD_DIM), BF16)
    in_specs = [row, vec, vec, _resident((VEC_ROWS, d)), _resident(w_in.shape, layer)]
    args = [h, shift, scale, g, w_in]
    out_shape = [heads_shape, jax.ShapeDtypeStruct((b, s, D_INNER), F32)]
    out_specs = [heads, pl.BlockSpec((ns, ts, D_INNER), lambda bi, si: (bi, si, 0))]
    if kv is not None:
        kshift, kscale, kg, w_kt, w_v = kv
        in_specs += [vec, vec, _resident((VEC_ROWS, d)), _resident(w_kt.shape), _resident(w_v.shape)]
        args += [kshift, kscale, kg, w_kt, w_v]
        out_shape += [jax.ShapeDtypeStruct((b, N_HEADS, HEAD_DIM, s), BF16), heads_shape]
        out_specs += [pl.BlockSpec((ns, N_HEADS, HEAD_DIM, ts), lambda bi, si: (bi, 0, 0, si)),
                      heads]
    return pl.pallas_call(
        functools.partial(_bproj_kernel, with_kv=kv is not None),
        out_shape=out_shape,
        grid=(b // ns, s // ts),
        in_specs=in_specs,
        out_specs=out_specs,
        compiler_params=pltpu.CompilerParams(
            dimension_semantics=("arbitrary", "arbitrary"), vmem_limit_bytes=VMEM_LIMIT),
        name="b_proj_kv" if kv is not None else "b_proj",
    )(*args)


def _attn_kernel(*refs, final):
    if final:
        (q_ref, k_ref, v_ref, z_ref, h_ref, gt_ref, bvec_ref, wout_ref, fg_ref,
         o_ref, kring, vring, att_ref, s_scr, p_scr, m_scr, tab_ref) = refs
    else:
        (q_ref, k_ref, v_ref, z_ref, h_ref, gt_ref, bvec_ref, wout_ref,
         o_ref, kring, vring, att_ref, s_scr, p_scr, m_scr, tab_ref) = refs
    i = pl.program_id(1)
    n_keys = KEY_BLOCKS * Q_BLOCK

    @pl.when((pl.program_id(0) == 0) & (i == 0))
    def _():
        def one_head(hd, carry):
            skew = pltpu.roll(jnp.broadcast_to(bvec_ref[hd], (Q_BLOCK, bvec_ref.shape[-1])),
                              0, 1, stride=1, stride_axis=0)
            qc = lax.broadcasted_iota(jnp.int32, (Q_BLOCK, n_keys), 0) // CHUNK
            kc = lax.broadcasted_iota(jnp.int32, (Q_BLOCK, n_keys), 1) // CHUNK
            bias = skew[:, :n_keys] * LOG2E
            bias = jnp.where(kc >= qc, bias, NEG_INF)
            tab_ref[hd] = jnp.where(kc <= qc + LEFT_CHUNKS, bias, NEG_INF)
            return carry

        lax.fori_loop(0, N_HEADS, one_head, 0)
        vring[:, :, :, HEAD_DIM:] = jnp.ones(vring.shape[:3] + (HEAD_DIM,), BF16)

    slots = [lax.rem(i + 1 + j, KEY_BLOCKS) for j in range(KEY_BLOCKS - 1)]
    ones = jnp.ones((Q_BLOCK, HEAD_DIM), BF16)

    def scores(hd, buf, live):
        qh = q_ref[0, hd]
        older = slots[KEY_BLOCKS - live:]
        keys_t = jnp.concatenate([kring[s, hd] for s in older] + [k_ref[0, hd]], axis=1)
        s = jnp.dot(qh, keys_t, preferred_element_type=F32)
        s = s + tab_ref[hd, :, (KEY_BLOCKS - live) * Q_BLOCK:]
        s_scr[buf, :, :live * Q_BLOCK] = s
        m_scr[buf] = functools.reduce(
            jnp.maximum, [s[:, c * LANES:(c + 1) * LANES] for c in range(s.shape[1] // LANES)])

    def softmax(buf, live):
        width = live * Q_BLOCK
        for r in range(0, Q_BLOCK, SOFTMAX_ROWS):
            rows = slice(r, r + SOFTMAX_ROWS)
            m = m_scr[buf, rows].max(axis=-1, keepdims=True)
            p_scr[buf, rows, :width] = jnp.exp2(s_scr[buf, rows, :width] - m).astype(BF16)

    def values(hd, buf, live):
        older = slots[KEY_BLOCKS - live:]
        own = jnp.concatenate([v_ref[0, hd], ones], axis=1)
        vals = jnp.concatenate([vring[s, hd] for s in older] + [own], axis=0)
        oe = jnp.dot(p_scr[buf, :, :live * Q_BLOCK], vals,
                     preferred_element_type=F32)
        att_ref[hd] = oe[:, :HEAD_DIM] * (1.0 / oe[:, HEAD_DIM:])

    def finish():
        kring[lax.rem(i, KEY_BLOCKS)] = k_ref[0]
        vring[lax.rem(i, KEY_BLOCKS), :, :, :HEAD_DIM] = v_ref[0]
        att = jnp.concatenate([att_ref[hd] for hd in range(N_HEADS)], axis=1)
        gz = att * _silu(z_ref[0])
        y = lax.dot_general(wout_ref[...], gz.astype(BF16), (((1,), (1,)), ((), ())),
                            preferred_element_type=F32).T
        out = h_ref[0] + gt_ref[0, 0:1] * y
        if final:
            out = out * _rms_scale(out) * fg_ref[0:1]
        o_ref[0] = out

    def all_heads(live):
        scores(0, 0, live)
        scores(1, 1, live)
        softmax(0, live)
        for hd in range(1, N_HEADS - 1):
            scores(hd + 1, (hd + 1) % 2, live)
            softmax(hd % 2, live)
            values(hd - 1, (hd - 1) % 2, live)
        softmax((N_HEADS - 1) % 2, live)
        values(N_HEADS - 2, (N_HEADS - 2) % 2, live)
        values(N_HEADS - 1, (N_HEADS - 1) % 2, live)
        finish()

    for live in range(1, KEY_BLOCKS):
        @pl.when(i == live - 1)
        def _(live=live):
            all_heads(live)

    @pl.when(i >= KEY_BLOCKS - 1)
    def _():
        all_heads(KEY_BLOCKS)


def _attn_call(q, k, v, z, h, gate, bias_vec, w_out, layer, final_g=None):
    b, s, d = h.shape
    tq = Q_BLOCK
    heads = pl.BlockSpec((1, N_HEADS, tq, HEAD_DIM), lambda bi, i: (bi, 0, i, 0))
    row = pl.BlockSpec((1, tq, d), lambda bi, i: (bi, i, 0))
    heads_t = pl.BlockSpec((1, N_HEADS, HEAD_DIM, tq), lambda bi, i: (bi, 0, 0, i))
    in_specs = [heads, heads_t, heads,
                pl.BlockSpec((1, tq, D_INNER), lambda bi, i: (bi, i, 0)),
                row,
                pl.BlockSpec((1, VEC_ROWS, d), lambda bi, i: (bi, 0, 0)),
                _resident(bias_vec.shape), _resident(w_out.shape, layer)]
    args = [q, k, v, z, h, gate, bias_vec, w_out]
    if final_g is not None:
        in_specs.append(_resident((VEC_ROWS, d)))
        args.append(final_g)
    kring = pltpu.VMEM((KEY_BLOCKS, N_HEADS, HEAD_DIM, tq), BF16)
    vring = pltpu.VMEM((KEY_BLOCKS, N_HEADS, tq, 2 * HEAD_DIM), BF16)
    return pl.pallas_call(
        functools.partial(_attn_kernel, final=final_g is not None),
        out_shape=jax.ShapeDtypeStruct(h.shape, F32),
        grid=(b, s // tq),
        in_specs=in_specs,
        out_specs=row,
        scratch_shapes=[kring, vring, pltpu.VMEM((N_HEADS, tq, HEAD_DIM), F32),
                        pltpu.VMEM((2, tq, KEY_BLOCKS * tq), F32),
                        pltpu.VMEM((2, tq, KEY_BLOCKS * tq), BF16),
                        pltpu.VMEM((2, tq, LANES), F32),
                        pltpu.VMEM((N_HEADS, tq, KEY_BLOCKS * tq), F32)],
        compiler_params=pltpu.CompilerParams(
            dimension_semantics=("arbitrary", "arbitrary"), vmem_limit_bytes=VMEM_LIMIT),
        name="band_attn_final" if final_g is not None else "band_attn",
    )(*args)


def _bias_vector(rel_bias):
    n_heads = rel_bias.shape[0]
    n_keys = KEY_BLOCKS * Q_BLOCK
    top = (KEY_BLOCKS - 1) * Q_BLOCK + Q_BLOCK - 1
    n_dist = Q_BLOCK + n_keys - 1
    vec = jnp.concatenate([
        jnp.broadcast_to(rel_bias[:, -1:], (n_heads, top - REL_CLIP)),
        rel_bias[:, ::-1],
        jnp.broadcast_to(rel_bias[:, :1], (n_heads, n_dist - (top - REL_CLIP) - rel_bias.shape[1])),
        jnp.zeros((n_heads, 1), rel_bias.dtype)], axis=1)
    return jnp.roll(vec, -(Q_BLOCK - 1), axis=1).astype(F32)[:, None, :]


@jax.jit
def _forward(x, c, ada_w, ada_b, norm_g, a_w_in, a_w_group, a_scale, a_w_out, kv_norm_g,
             kv_ada_w, kv_ada_b, w_kv, b_w_in, b_rel_bias, b_w_out, final_g):
    b, s, d = x.shape
    depth = ada_w.shape[0]
    n_a = a_w_in.shape[0]
    assert depth > n_a
    c_pad = jnp.pad(c, ((0, 8 - b), (0, 0)))
    mods = _ada_call(c_pad, ada_w, ada_b.reshape(depth, 1, 3 * d))[:, :b]
    kvm = _ada_call(c_pad, kv_ada_w[None], kv_ada_b.reshape(1, 1, 2 * d))[0, :b]

    def vec(a):
        return jnp.broadcast_to(a[:, None, :], (b, VEC_ROWS, a.shape[-1]))

    def rows(a):
        return jnp.broadcast_to(a[None, :], (VEC_ROWS, a.shape[-1]))

    a_w_in = _fold_call(a_w_in, a_w_group)
    a_w_out = a_w_out.astype(BF16)
    b_w_in = b_w_in.astype(BF16)
    b_w_out_t = jnp.swapaxes(b_w_out, 1, 2).astype(BF16)
    w_kt = w_kv[:, :D_INNER].T.astype(BF16)
    w_v = w_kv[:, D_INNER:].astype(BF16)

    h = x
    k = v = None
    for layer in range(depth):
        shift, scale, gate = (vec(mods[layer, :, j * d:(j + 1) * d]) for j in range(3))
        g = rows(norm_g[layer])
        if layer < n_a:
            h = _a_layer_call(h, shift, scale, gate, g, a_w_in,
                              rows(a_scale[layer]), a_w_out, layer)
        else:
            bi = layer - n_a
            if layer == n_a:
                kv = (vec(kvm[:, :d]), vec(kvm[:, d:]), rows(kv_norm_g), w_kt, w_v)
                q, z, k, v = _bproj_call(h, shift, scale, g, b_w_in, bi, kv)
            else:
                q, z = _bproj_call(h, shift, scale, g, b_w_in, bi)
            last = layer == depth - 1
            h = _attn_call(q, k, v, z, h, gate, _bias_vector(b_rel_bias[bi]), b_w_out_t, bi,
                           rows(final_g) if last else None)
    return h


def kernel(x, c, ada_w, ada_b, norm_g, a_w_in, a_w_group, a_scale, a_w_out, kv_norm_g, kv_ada_w, kv_ada_b, w_kv, b_w_in, b_rel_bias, b_w_out, final_g):
    return _forward(x, c, ada_w, ada_b, norm_g, a_w_in, a_w_group, a_scale, a_w_out,
                    kv_norm_g, kv_ada_w, kv_ada_b, w_kv, b_w_in, b_rel_bias, b_w_out, final_g)
```

```python
import functools

import jax
import jax.numpy as jnp
from jax import lax
from jax.experimental import pallas as pl
from jax.experimental.pallas import tpu as pltpu

F32 = jnp.float32
BF16 = jnp.bfloat16

D_MODEL = 1024
D_INNER = 2048
N_HEADS = 16
HEAD_DIM = 128
CHUNK = 64
LEFT_CHUNKS = 8
REL_CLIP = 128
POOL_WINDOWS = (2, 4, 8, 16)
POOL_GROUP_W = D_INNER // len(POOL_WINDOWS)
POOL_HALO = 16
EPS = 1e-6
LOG2E = 1.4426950408889634
Q_PRESCALE = HEAD_DIM ** -0.5 * LOG2E
NEG_INF = float("-inf")

LANES = 128
Q_BLOCK = 256
KEY_BLOCKS = LEFT_CHUNKS * CHUNK // Q_BLOCK + 1
SOFTMAX_ROWS = 64
A_TILE = 512
A_STREAMS = 2
P_TILE = 512
P_STREAMS = 2
VMEM_LIMIT = 54 * 1024 * 1024


def _silu(x):
    return x * (1.0 / (1.0 + jnp.exp(-x)))


def _rms_scale(x):
    return lax.rsqrt(jnp.mean(x * x, axis=-1, keepdims=True) + EPS)


def _modulated(x, g, shift, scale):
    return ((x * _rms_scale(x)) * (g * (1.0 + scale)) + shift).astype(BF16)


def _resident(shape, layer=None):
    if layer is None:
        return pl.BlockSpec(shape, lambda *_: (0,) * len(shape), pipeline_mode=pl.Buffered(1))
    return pl.BlockSpec((None,) + tuple(shape[1:]),
                        lambda *_: (layer,) + (0,) * (len(shape) - 1),
                        pipeline_mode=pl.Buffered(1))


def _ada_kernel(c_ref, w_ref, b_ref, o_ref):
    ca = _silu(c_ref[...])
    o_ref[0] = jnp.dot(ca, w_ref[0], preferred_element_type=F32) + b_ref[0]


def _ada_call(c_pad, w, b):
    n_layers, d, n = w.shape
    return pl.pallas_call(
        _ada_kernel,
        out_shape=jax.ShapeDtypeStruct((n_layers, c_pad.shape[0], n), F32),
        grid=(n_layers,),
        in_specs=[
            pl.BlockSpec(c_pad.shape, lambda l: (0, 0)),
            pl.BlockSpec((1, d, n), lambda l: (l, 0, 0)),
            pl.BlockSpec((1, 1, n), lambda l: (l, 0, 0)),
        ],
        out_specs=pl.BlockSpec((1, c_pad.shape[0], n), lambda l: (l, 0, 0)),
        compiler_params=pltpu.CompilerParams(
            dimension_semantics=("arbitrary",), vmem_limit_bytes=VMEM_LIMIT),
        name="ada_mod",
    )(c_pad, w, b)


def _fold_kernel(win_ref, wg_ref, o_ref, *, n_groups):
    j = pl.program_id(1)

    def split(x):
        hi = x.astype(BF16)
        return hi, (x - hi.astype(F32)).astype(BF16)

    @pl.when(j < n_groups)
    def _():
        a_hi, a_lo = split(win_ref[0])
        b_hi, b_lo = split(wg_ref[0, 0])
        prod = (jnp.dot(a_hi, b_hi, preferred_element_type=F32)
                + jnp.dot(a_lo, b_hi, preferred_element_type=F32)
                + jnp.dot(a_hi, b_lo, preferred_element_type=F32))
        o_ref[0] = prod.astype(BF16)

    @pl.when(j >= n_groups)
    def _():
        o_ref[0] = win_ref[0].astype(BF16)


def _fold_call(w_in, w_group):
    n_layers, d, n = w_in.shape
    n_groups, gw = w_group.shape[1], w_group.shape[2]
    return pl.pallas_call(
        functools.partial(_fold_kernel, n_groups=n_groups),
        out_shape=jax.ShapeDtypeStruct(w_in.shape, BF16),
        grid=(n_layers, n // gw),
        in_specs=[pl.BlockSpec((1, d, gw), lambda l, j: (l, 0, j)),
                  pl.BlockSpec((1, 1, gw, gw), lambda l, j: (l, jnp.minimum(j, n_groups - 1), 0, 0))],
        out_specs=pl.BlockSpec((1, d, gw), lambda l, j: (l, 0, j)),
        compiler_params=pltpu.CompilerParams(
            dimension_semantics=("arbitrary", "arbitrary")),
        name="fold_group_mix",
    )(w_in, w_group)


def _a_kernel(h_ref, sh_ref, sc_ref, gt_ref, g_ref, win_ref, asc_ref, wout_ref,
              o_ref, carry_ref, u_ref, hin_ref, hout_ref):
    n_streams, ts = h_ref.shape[0], h_ref.shape[1]
    streams = range(n_streams)
    si = pl.program_id(1)

    @pl.when(si == 0)
    def _():
        carry_ref[...] = jnp.zeros_like(carry_ref)

    rows = [slice(n * ts, (n + 1) * ts) for n in streams]
    nv = ts // 8
    n_ct = h_ref.shape[2] // LANES
    for n in streams:
        for c in range(n_ct):
            hin_ref[n, c] = h_ref[n, :, c * LANES:(c + 1) * LANES]

    def strided_rows(n):
        return jnp.concatenate(
            [jnp.concatenate([hin_ref[n, c, pl.ds(j, 8, stride=nv), :] for c in range(n_ct)], axis=1)
             for j in range(nv)], axis=0)

    for n in streams:
        u_ref[rows[n]] = _modulated(strided_rows(n), g_ref[...], sh_ref[n], sc_ref[n])

    groups = sorted(range(len(POOL_WINDOWS)), key=lambda g: -POOL_WINDOWS[g])
    span = {g: slice(g * POOL_GROUP_W, (g + 1) * POOL_GROUP_W) for g in groups}
    val = {g: jnp.dot(u_ref[...], win_ref[:, span[g]], preferred_element_type=F32)
           for g in groups}
    r = lax.broadcasted_iota(jnp.int32, (ts, LANES), 0)
    t = si * ts + (r % 8) * nv + r // 8
    halo = 8 * POOL_HALO
    first_sublane = jnp.concatenate(
        [lax.broadcasted_iota(jnp.int32, (halo, LANES), 0) % 8 == 0] * (POOL_GROUP_W // LANES), axis=1)
    gz = {}
    for g in groups:
        w = POOL_WINDOWS[g]
        inv = 1.0 / jnp.minimum(t + 1, w).astype(F32)
        inv = jnp.concatenate([inv] * (POOL_GROUP_W // LANES), axis=1)
        z_all = jnp.dot(u_ref[...], win_ref[:, D_INNER + span[g].start:D_INNER + span[g].stop],
                        preferred_element_type=F32)
        for n in streams:
            z = z_all[rows[n]]
            v = val[g][rows[n]]
            last = v[ts - halo:, :]
            before = jnp.where(first_sublane,
                               pltpu.roll(carry_ref[n, :, span[g]], halo - 7, axis=0),
                               pltpu.roll(last, 1, axis=0))
            carry_ref[n, :, span[g]] = last
            s = jnp.concatenate([before, v], axis=0)
            k = 1
            while k < w:
                s = s + pltpu.roll(s, 8 * k, axis=0)
                k *= 2
            mixed = s[halo:, :] * inv - v
            gz[n, g] = ((mixed * asc_ref[:, span[g]]) * _silu(z)).astype(BF16)
    gated = jnp.concatenate(
        [jnp.concatenate([gz[n, g] for g in sorted(groups)], axis=1) for n in streams], axis=0)
    y = jnp.dot(gated, wout_ref[...], preferred_element_type=F32)
    for n in streams:
        out = strided_rows(n) + gt_ref[n] * y[rows[n]]
        for j in range(nv):
            for c in range(n_ct):
                hout_ref[c, pl.ds(j, 8, stride=nv), :] = out[j * 8:(j + 1) * 8,
                                                             c * LANES:(c + 1) * LANES]
        for c in range(n_ct):
            o_ref[n, :, c * LANES:(c + 1) * LANES] = hout_ref[c]


def _a_layer_call(h, shift, scale, gate, g, w_in, a_scale, w_out, layer):
    b, s, d = h.shape
    ts, ns = A_TILE, A_STREAMS
    assert b % ns == 0 and s % ts == 0
    row = pl.BlockSpec((ns, ts, d), lambda bi, si: (bi, si, 0))
    vec = pl.BlockSpec((ns, 1, d), lambda bi, si: (bi, 0, 0))
    return pl.pallas_call(
        _a_kernel,
        out_shape=jax.ShapeDtypeStruct(h.shape, F32),
        grid=(b // ns, s // ts),
        in_specs=[row, vec, vec, vec,
                  _resident((1, d)), _resident(w_in.shape, layer),
                  _resident((1, D_INNER)), _resident(w_out.shape, layer)],
        out_specs=row,
        scratch_shapes=[pltpu.VMEM((ns, 8 * POOL_HALO, D_INNER), F32),
                        pltpu.VMEM((ns * ts, d), BF16),
                        pltpu.VMEM((ns, d // LANES, ts, LANES), F32),
                        pltpu.VMEM((d // LANES, ts, LANES), F32)],
        compiler_params=pltpu.CompilerParams(
            dimension_semantics=("arbitrary", "arbitrary"),
            vmem_limit_bytes=VMEM_LIMIT + 2 * 1024 * 1024),
        name="a_layer",
    )(h, shift, scale, gate, g, w_in, a_scale, w_out)


def _store_heads(dst_ref, n, val):
    for hd in range(N_HEADS):
        dst_ref[n, hd] = val[:, hd * HEAD_DIM:(hd + 1) * HEAD_DIM].astype(BF16)


def _bproj_kernel(*refs, with_kv):
    if with_kv:
        (h_ref, sh_ref, sc_ref, g_ref, win_ref, ksh_ref, ksc_ref, kg_ref, wkt_ref, wv_ref,
         q_ref, z_ref, kt_ref, v_ref) = refs
    else:
        h_ref, sh_ref, sc_ref, g_ref, win_ref, q_ref, z_ref = refs
    n_streams, ts = h_ref.shape[0], h_ref.shape[1]
    streams = range(n_streams)
    rows = [slice(n * ts, (n + 1) * ts) for n in streams]
    xn = [h_ref[n] * _rms_scale(h_ref[n]) for n in streams]
    u = jnp.concatenate([(xn[n] * (g_ref[...] * (1.0 + sc_ref[n])) + sh_ref[n]).astype(BF16)
                         for n in streams], axis=0)
    q = jnp.dot(u, win_ref[:, :D_INNER], preferred_element_type=F32)
    for n in streams:
        _store_heads(q_ref, n, q[rows[n]] * Q_PRESCALE)
    z = jnp.dot(u, win_ref[:, D_INNER:], preferred_element_type=F32)
    for n in streams:
        z_ref[n] = z[rows[n]]
    if with_kv:
        hk = jnp.concatenate([(xn[n] * (kg_ref[...] * (1.0 + ksc_ref[n])) + ksh_ref[n]).astype(BF16)
                              for n in streams], axis=0)
        v = jnp.dot(hk, wv_ref[...], preferred_element_type=F32)
        kt = lax.dot_general(wkt_ref[...], hk, (((1,), (1,)), ((), ())),
                             preferred_element_type=F32)
        for n in streams:
            _store_heads(v_ref, n, v[rows[n]])
            for hd in range(N_HEADS):
                kt_ref[n, hd] = kt[hd * HEAD_DIM:(hd + 1) * HEAD_DIM, rows[n]].astype(BF16)


def _bproj_call(h, shift, scale, g, w_in, layer, kv=None):
    b, s, d = h.shape
    ts = P_TILE
    ns = 1 if kv is not None else P_STREAMS
    assert b % ns == 0 and s % ts == 0
    row = pl.BlockSpec((ns, ts, d), lambda bi, si: (bi, si, 0))
    vec = pl.BlockSpec((ns, 1, d), lambda bi, si: (bi, 0, 0))
    heads = pl.BlockSpec((ns, N_HEADS, ts, HEAD_DIM), lambda bi, si: (bi, 0, si, 0))
    heads_shape = jax.ShapeDtypeStruct((b, N_HEADS, s, HEAD_DIM), BF16)
    in_specs = [row, vec, vec, _resident((1, d)), _resident(w_in.shape, layer)]
    args = [h, shift, scale, g, w_in]
    out_shape = [heads_shape, jax.ShapeDtypeStruct((b, s, D_INNER), F32)]
    out_specs = [heads, pl.BlockSpec((ns, ts, D_INNER), lambda bi, si: (bi, si, 0))]
    if kv is not None:
        kshift, kscale, kg, w_kt, w_v = kv
        in_specs += [vec, vec, _resident((1, d)), _resident(w_kt.shape), _resident(w_v.shape)]
        args += [kshift, kscale, kg, w_kt, w_v]
        out_shape += [jax.ShapeDtypeStruct((b, N_HEADS, HEAD_DIM, s), BF16), heads_shape]
        out_specs += [pl.BlockSpec((ns, N_HEADS, HEAD_DIM, ts), lambda bi, si: (bi, 0, 0, si)),
                      heads]
    return pl.pallas_call(
        functools.partial(_bproj_kernel, with_kv=kv is not None),
        out_shape=out_shape,
        grid=(b // ns, s // ts),
        in_specs=in_specs,
        out_specs=out_specs,
        compiler_params=pltpu.CompilerParams(
            dimension_semantics=("arbitrary", "arbitrary"), vmem_limit_bytes=VMEM_LIMIT),
        name="b_proj_kv" if kv is not None else "b_proj",
    )(*args)


def _attn_kernel(*refs, final):
    if final:
        (q_ref, k_ref, v_ref, z_ref, h_ref, gt_ref, bvec_ref, wout_ref, fg_ref,
         o_ref, kring, vring, att_ref, s_scr, p_scr, m_scr, tab_ref) = refs
    else:
        (q_ref, k_ref, v_ref, z_ref, h_ref, gt_ref, bvec_ref, wout_ref,
         o_ref, kring, vring, att_ref, s_scr, p_scr, m_scr, tab_ref) = refs
    i = pl.program_id(1)
    n_keys = KEY_BLOCKS * Q_BLOCK

    @pl.when((pl.program_id(0) == 0) & (i == 0))
    def _():
        def one_head(hd, carry):
            skew = pltpu.roll(jnp.broadcast_to(bvec_ref[hd], (Q_BLOCK, bvec_ref.shape[-1])),
                              0, 1, stride=1, stride_axis=0)
            qc = lax.broadcasted_iota(jnp.int32, (Q_BLOCK, n_keys), 0) // CHUNK
            kc = lax.broadcasted_iota(jnp.int32, (Q_BLOCK, n_keys), 1) // CHUNK
            bias = skew[:, :n_keys] * LOG2E
            bias = jnp.where(kc >= qc, bias, NEG_INF)
            tab_ref[hd] = jnp.where(kc <= qc + LEFT_CHUNKS, bias, NEG_INF)
            return carry

        lax.fori_loop(0, N_HEADS, one_head, 0)
        vring[:, :, :, HEAD_DIM:] = jnp.ones(vring.shape[:3] + (HEAD_DIM,), BF16)

    slots = [lax.rem(i + 1 + j, KEY_BLOCKS) for j in range(KEY_BLOCKS - 1)]
    ones = jnp.ones((Q_BLOCK, HEAD_DIM), BF16)

    def scores(hd, buf, live):
        qh = q_ref[0, hd]
        older = slots[KEY_BLOCKS - live:]
        keys_t = jnp.concatenate([kring[s, hd] for s in older] + [k_ref[0, hd]], axis=1)
        s = jnp.dot(qh, keys_t, preferred_element_type=F32)
        s = s + tab_ref[hd, :, (KEY_BLOCKS - live) * Q_BLOCK:]
        s_scr[buf, :, :live * Q_BLOCK] = s
        m_scr[buf] = functools.reduce(
            jnp.maximum, [s[:, c * LANES:(c + 1) * LANES] for c in range(s.shape[1] // LANES)])

    def softmax(buf, live):
        width = live * Q_BLOCK
        for r in range(0, Q_BLOCK, SOFTMAX_ROWS):
            rows = slice(r, r + SOFTMAX_ROWS)
            m = m_scr[buf, rows].max(axis=-1, keepdims=True)
            p_scr[buf, rows, :width] = jnp.exp2(s_scr[buf, rows, :width] - m).astype(BF16)

    def values(hd, buf, live):
        older = slots[KEY_BLOCKS - live:]
        own = jnp.concatenate([v_ref[0, hd], ones], axis=1)
        vals = jnp.concatenate([vring[s, hd] for s in older] + [own], axis=0)
        oe = jnp.dot(p_scr[buf, :, :live * Q_BLOCK], vals,
                     preferred_element_type=F32)
        att_ref[hd] = oe[:, :HEAD_DIM] * (1.0 / oe[:, HEAD_DIM:])

    def finish():
        kring[lax.rem(i, KEY_BLOCKS)] = k_ref[0]
        vring[lax.rem(i, KEY_BLOCKS), :, :, :HEAD_DIM] = v_ref[0]
        att = jnp.concatenate([att_ref[hd] for hd in range(N_HEADS)], axis=1)
        gz = att * _silu(z_ref[0])
        y = lax.dot_general(wout_ref[...], gz.astype(BF16), (((1,), (1,)), ((), ())),
                            preferred_element_type=F32).T
        out = h_ref[0] + gt_ref[0] * y
        if final:
            out = out * _rms_scale(out) * fg_ref[...]
        o_ref[0] = out

    def all_heads(live):
        scores(0, 0, live)
        scores(1, 1, live)
        softmax(0, live)
        for hd in range(1, N_HEADS - 1):
            scores(hd + 1, (hd + 1) % 2, live)
            softmax(hd % 2, live)
            values(hd - 1, (hd - 1) % 2, live)
        softmax((N_HEADS - 1) % 2, live)
        values(N_HEADS - 2, (N_HEADS - 2) % 2, live)
        values(N_HEADS - 1, (N_HEADS - 1) % 2, live)
        finish()

    for live in range(1, KEY_BLOCKS):
        @pl.when(i == live - 1)
        def _(live=live):
            all_heads(live)

    @pl.when(i >= KEY_BLOCKS - 1)
    def _():
        all_heads(KEY_BLOCKS)


def _attn_call(q, k, v, z, h, gate, bias_vec, w_out, layer, final_g=None):
    b, s, d = h.shape
    tq = Q_BLOCK
    heads = pl.BlockSpec((1, N_HEADS, tq, HEAD_DIM), lambda bi, i: (bi, 0, i, 0))
    row = pl.BlockSpec((1, tq, d), lambda bi, i: (bi, i, 0))
    heads_t = pl.BlockSpec((1, N_HEADS, HEAD_DIM, tq), lambda bi, i: (bi, 0, 0, i))
    in_specs = [heads, heads_t, heads,
                pl.BlockSpec((1, tq, D_INNER), lambda bi, i: (bi, i, 0)),
                row,
                pl.BlockSpec((1, 1, d), lambda bi, i: (bi, 0, 0)),
                _resident(bias_vec.shape), _resident(w_out.shape, layer)]
    args = [q, k, v, z, h, gate, bias_vec, w_out]
    if final_g is not None:
        in_specs.append(_resident((1, d)))
        args.append(final_g)
    kring = pltpu.VMEM((KEY_BLOCKS, N_HEADS, HEAD_DIM, tq), BF16)
    vring = pltpu.VMEM((KEY_BLOCKS, N_HEADS, tq, 2 * HEAD_DIM), BF16)
    return pl.pallas_call(
        functools.partial(_attn_kernel, final=final_g is not None),
        out_shape=jax.ShapeDtypeStruct(h.shape, F32),
        grid=(b, s // tq),
        in_specs=in_specs,
        out_specs=row,
        scratch_shapes=[kring, vring, pltpu.VMEM((N_HEADS, tq, HEAD_DIM), F32),
                        pltpu.VMEM((2, tq, KEY_BLOCKS * tq), F32),
                        pltpu.VMEM((2, tq, KEY_BLOCKS * tq), BF16),
                        pltpu.VMEM((2, tq, LANES), F32),
                        pltpu.VMEM((N_HEADS, tq, KEY_BLOCKS * tq), F32)],
        compiler_params=pltpu.CompilerParams(
            dimension_semantics=("arbitrary", "arbitrary"), vmem_limit_bytes=VMEM_LIMIT),
        name="band_attn_final" if final_g is not None else "band_attn",
    )(*args)


def _bias_vector(rel_bias):
    n_heads = rel_bias.shape[0]
    n_keys = KEY_BLOCKS * Q_BLOCK
    top = (KEY_BLOCKS - 1) * Q_BLOCK + Q_BLOCK - 1
    n_dist = Q_BLOCK + n_keys - 1
    vec = jnp.concatenate([
        jnp.broadcast_to(rel_bias[:, -1:], (n_heads, top - REL_CLIP)),
        rel_bias[:, ::-1],
        jnp.broadcast_to(rel_bias[:, :1], (n_heads, n_dist - (top - REL_CLIP) - rel_bias.shape[1])),
        jnp.zeros((n_heads, 1), rel_bias.dtype)], axis=1)
    return jnp.roll(vec, -(Q_BLOCK - 1), axis=1).astype(F32)[:, None, :]


@jax.jit
def _forward(x, c, ada_w, ada_b, norm_g, a_w_in, a_w_group, a_scale, a_w_out, kv_norm_g,
             kv_ada_w, kv_ada_b, w_kv, b_w_in, b_rel_bias, b_w_out, final_g):
    b, s, d = x.shape
    depth = ada_w.shape[0]
    n_a = a_w_in.shape[0]
    assert depth > n_a
    c_pad = jnp.pad(c, ((0, 8 - b), (0, 0)))
    mods = _ada_call(c_pad, ada_w, ada_b.reshape(depth, 1, 3 * d))[:, :b]
    kvm = _ada_call(c_pad, kv_ada_w[None], kv_ada_b.reshape(1, 1, 2 * d))[0, :b]

    def vec(a):
        return a.reshape(b, 1, d)

    a_w_in = _fold_call(a_w_in, a_w_group)
    a_w_out = a_w_out.astype(BF16)
    b_w_in = b_w_in.astype(BF16)
    b_w_out_t = jnp.swapaxes(b_w_out, 1, 2).astype(BF16)
    w_kt = w_kv[:, :D_INNER].T.astype(BF16)
    w_v = w_kv[:, D_INNER:].astype(BF16)

    h = x
    k = v = None
    for layer in range(depth):
        shift, scale, gate = (vec(mods[layer, :, j * d:(j + 1) * d]) for j in range(3))
        g = norm_g[layer].reshape(1, d)
        if layer < n_a:
            h = _a_layer_call(h, shift, scale, gate, g, a_w_in,
                              a_scale[layer].reshape(1, D_INNER), a_w_out, layer)
        else:
            bi = layer - n_a
            if layer == n_a:
                kv = (vec(kvm[:, :d]), vec(kvm[:, d:]), kv_norm_g.reshape(1, d), w_kt, w_v)
                q, z, k, v = _bproj_call(h, shift, scale, g, b_w_in, bi, kv)
            else:
                q, z = _bproj_call(h, shift, scale, g, b_w_in, bi)
            last = layer == depth - 1
            h = _attn_call(q, k, v, z, h, gate, _bias_vector(b_rel_bias[bi]), b_w_out_t, bi,
                           final_g.reshape(1, d) if last else None)
    return h


def kernel(x, c, ada_w, ada_b, norm_g, a_w_in, a_w_group, a_scale, a_w_out, kv_norm_g, kv_ada_w, kv_ada_b, w_kv, b_w_in, b_rel_bias, b_w_out, final_g):
    return _forward(x, c, ada_w, ada_b, norm_g, a_w_in, a_w_group, a_scale, a_w_out,
                    kv_norm_g, kv_ada_w, kv_ada_b, w_kv, b_w_in, b_rel_bias, b_w_out, final_g)
```
